```python
import math
import jax, jax.numpy as jnp
from jax import lax
import numpy as np

D_MODEL = 4096
BATCH = 1
SEQ = 8192
DEPTH = 4

CTX_LEN = 256
GRID_W = 64
EPS = 1e-6
NEG_INF = -1e30
ROPE_BASE = 10000.0
Q_BLOCK = 128
ADA_RANK = 256
N_MOD = 6
MIX_HALF = D_MODEL // 2

A_DK = 256
A_DV = 512
A_HEADS = MIX_HALF // A_DV
A_CHUNK = 64
F_GATE_BIAS = 3.0
B_VDIM = 128
B_HEADS = MIX_HALF // B_VDIM
B_NOPE = 128
B_ROPE = 64
B_Q_RANK = 1024
B_KV_RANK = 512
C_DIM = 128
C_HEADS = MIX_HALF // C_DIM
C_WIN_R = 8
C_WIN_C = 16
D_DIM = 128
D_VDIM = 2 * D_DIM
D_HEADS = MIX_HALF // D_VDIM
N_EXPERTS = 16
N_GROUPS = 4
TOP_K = 2
D_EXPERT = 768
MOE_BLOCK = 128

A_IN_SIZES = (A_HEADS * A_DK, A_HEADS * A_DK, A_HEADS * A_DV, A_HEADS * A_DV, 4 * A_HEADS, B_Q_RANK, B_KV_RANK, B_ROPE)
B_IN_SIZES = (C_HEADS * C_DIM, C_HEADS * C_DIM, C_HEADS * C_DIM, 2 * D_HEADS * D_DIM, 2 * D_HEADS * D_DIM, D_HEADS * D_VDIM)

kernel_name = 'hybrid_mlstm_mla_natten_diff_moe_dit'


def rmsnorm(x, w, eps=EPS):
    xf = x.astype(jnp.float32)
    y = xf * lax.rsqrt(jnp.mean(xf * xf, axis=-1, keepdims=True) + eps)
    return (y * w.astype(jnp.float32)).astype(x.dtype)


def modulate(x, w, shift, scale):
    return rmsnorm(x, w) * (1.0 + scale) + shift


def ada_mod(cvec, down, up, bias):
    return jnp.split((jax.nn.silu(cvec) @ down) @ up + bias, N_MOD, axis=-1)


def split_cols(p, sizes):
    idx, acc = [], 0
    for s in sizes[:-1]:
        acc += s
        idx.append(acc)
    return jnp.split(p, idx, axis=-1)


def to_heads(a, h, d):
    return a.reshape(a.shape[0], a.shape[1], h, d)


def rope_1d(x, pos):
    half = x.shape[-1] // 2
    inv = ROPE_BASE ** (-jnp.arange(half, dtype=jnp.float32) / half)
    ang = pos.astype(jnp.float32)[:, None] * inv[None, :]
    cos, sin = jnp.cos(ang)[:, None, :], jnp.sin(ang)[:, None, :]
    x1, x2 = x[..., :half], x[..., half:]
    return jnp.concatenate([x1 * cos - x2 * sin, x2 * cos + x1 * sin], axis=-1)


def axial_rope(x, row, col):
    xf = x.astype(jnp.float32)
    h = x.shape[-1] // 2
    return jnp.concatenate([rope_1d(xf[..., :h], row), rope_1d(xf[..., h:], col)], axis=-1).astype(x.dtype)


def over_query_blocks(fn, q):
    B, S = q.shape[:2]
    nb = S // Q_BLOCK
    qb = jnp.moveaxis(q.reshape(B, nb, Q_BLOCK, *q.shape[2:]), 1, 0)
    out = lax.map(fn, qb)
    return jnp.moveaxis(out, 0, 1).reshape(B, S, *out.shape[3:])


def softmax_attn(q, k, v):
    s = jnp.einsum('bqhd,bkhd->bhqk', q, k, preferred_element_type=jnp.float32) * (q.shape[-1] ** -0.5)
    p = jax.nn.softmax(s, axis=-1).astype(v.dtype)
    return jnp.einsum('bhqk,bkhd->bqhd', p, v)


def diff_attn(q, k, v, lam):
    s = jnp.einsum('bqhmd,bkhmd->bhmqk', q, k, preferred_element_type=jnp.float32) * (q.shape[-1] ** -0.5)
    p = jax.nn.softmax(s, axis=-1)
    pd = (p[:, :, 0] - lam * p[:, :, 1]).astype(v.dtype)
    return jnp.einsum('bhqk,bkhv->bqhv', pd, v)


def diff_post(o, w, lam_init):
    B, N = o.shape[:2]
    return (rmsnorm(o, w) * (1.0 - lam_init)).reshape(B, N, -1)


def neighbourhood_attention(q, k, v, k_ctx, v_ctx, rpb):
    B, S, H, d = q.shape
    rows = S // GRID_W
    kr = min(C_WIN_R, rows)
    L = k_ctx.shape[1]
    qg = q.reshape(B, rows, GRID_W, H, d)
    kg = k.reshape(B, rows, GRID_W, H, d)
    vg = v.reshape(B, rows, GRID_W, H, d)
    cols = jnp.arange(GRID_W)
    c0 = jnp.clip(cols - C_WIN_C // 2, 0, GRID_W - C_WIN_C)
    col_ok = (cols[None, :] >= c0[:, None]) & (cols[None, :] < c0[:, None] + C_WIN_C)
    dc_idx = jnp.clip(cols[None, :] - cols[:, None] + C_WIN_C - 1, 0, 2 * C_WIN_C - 2)
    bias_c = rpb.astype(jnp.float32)[:, :, dc_idx]
    scale = d ** -0.5

    def row_block(i):
        r0 = jnp.clip(i - kr // 2, 0, rows - kr)
        kw = lax.dynamic_slice_in_dim(kg, r0, kr, axis=1).reshape(B, kr * GRID_W, H, d)
        vw = lax.dynamic_slice_in_dim(vg, r0, kr, axis=1).reshape(B, kr * GRID_W, H, d)
        qi = lax.dynamic_index_in_dim(qg, i, axis=1, keepdims=False)
        s_loc = jnp.einsum('bqhd,bkhd->bhqk', qi, kw, preferred_element_type=jnp.float32) * scale
        s_loc = s_loc.reshape(B, H, GRID_W, kr, GRID_W)
        dr = r0 + jnp.arange(kr) - i + C_WIN_R - 1
        bias = jnp.take(bias_c, dr, axis=1).transpose(0, 2, 1, 3)
        s_loc = jnp.where(col_ok[:, None, :], s_loc + bias, NEG_INF)
        s_ctx = jnp.einsum('bqhd,bkhd->bhqk', qi, k_ctx, preferred_element_type=jnp.float32) * scale
        s = jnp.concatenate([s_ctx, s_loc.reshape(B, H, GRID_W, kr * GRID_W)], axis=-1)
        p = jax.nn.softmax(s, axis=-1).astype(v.dtype)
        return (jnp.einsum('bhqk,bkhd->bqhd', p[..., :L], v_ctx)
                + jnp.einsum('bhqk,bkhd->bqhd', p[..., L:], vw))

    out = lax.map(row_block, jnp.arange(rows))
    return jnp.moveaxis(out, 0, 1).reshape(B, S, H, d)


def mlstm_prep(q, k, v, gates, gate_bias):
    B, N, _ = q.shape
    f32 = jnp.float32
    qh = to_heads(q.astype(f32), A_HEADS, A_DK).transpose(0, 2, 1, 3) * (A_DK ** -0.5)
    kh = to_heads(k.astype(f32), A_HEADS, A_DK).transpose(0, 2, 1, 3)
    vh = to_heads(v.astype(f32), A_HEADS, A_DV).transpose(0, 2, 1, 3)
    g = (gates.astype(f32) + gate_bias.astype(f32)).reshape(B, N, 4, A_HEADS).transpose(2, 0, 3, 1)
    return (qh, kh, vh, g[0], jax.nn.log_sigmoid(g[1]), g[2], jax.nn.log_sigmoid(g[3]))


def mlstm_scan(q, k, v, ig, lf, state, emit):
    B, H, N, _ = q.shape
    dv = v.shape[-1]
    nc = N // A_CHUNK

    def chunked(a):
        return jnp.moveaxis(a.reshape(B, H, nc, A_CHUNK, *a.shape[3:]), 2, 0)

    xs = tuple(chunked(a) for a in (q, k, v, ig, lf))
    causal = jnp.tril(jnp.ones((A_CHUNK, A_CHUNK), bool))

    def body(carry, inp):
        C, n, m = carry
        qc, kc, vc, ic, fc = inp
        b = jnp.cumsum(fc, axis=-1)
        b_last = b[..., -1]
        w_end = b_last[..., None] - b + ic
        m_new = jnp.maximum(b_last + m, jnp.max(w_end, axis=-1))
        decay = jnp.exp(b_last + m - m_new)
        ws = jnp.exp(w_end - m_new[..., None])
        C_new = decay[..., None, None] * C + jnp.einsum('bhsv,bhsd->bhvd', vc * ws[..., None], kc)
        n_new = decay[..., None] * n + jnp.einsum('bhs,bhsd->bhd', ws, kc)
        if not emit:
            return (C_new, n_new, m_new), None
        logw = jnp.where(causal, b[..., :, None] - b[..., None, :] + ic[..., None, :], NEG_INF)
        inter = b + m[..., None]
        m_row = jnp.maximum(inter, jnp.max(logw, axis=-1))
        A = jnp.exp(logw - m_row[..., None]) * jnp.einsum('bhjd,bhsd->bhjs', qc, kc)
        e_inter = jnp.exp(inter - m_row)
        num = jnp.einsum('bhjs,bhsv->bhjv', A, vc) + e_inter[..., None] * jnp.einsum('bhvd,bhjd->bhjv', C, qc)
        den = jnp.sum(A, axis=-1) + e_inter * jnp.einsum('bhd,bhjd->bhj', n, qc)
        h = num / jnp.maximum(jnp.abs(den), jnp.exp(-m_row))[..., None]
        return (C_new, n_new, m_new), h

    state, hs = lax.scan(body, state, xs)
    h = jnp.moveaxis(hs, 0, 2).reshape(B, H, N, dv) if emit else None
    return state, h


def bidir_mlstm(mc, mx, need_ctx):
    qc, kc, vc, icf, fcf, icb, fcb = mc
    qx, kx, vx, ixf, fxf, ixb, fxb = mx
    B, H = qc.shape[:2]
    f32 = jnp.float32
    st0 = (jnp.zeros((B, H, A_DV, A_DK), f32), jnp.zeros((B, H, A_DK), f32), jnp.zeros((B, H), f32))

    def rev(*arrs):
        return tuple(jnp.flip(a, axis=2) for a in arrs)

    st_f, hcf = mlstm_scan(qc, kc, vc, icf, fcf, st0, need_ctx)
    st_b, hcb = mlstm_scan(*rev(qc, kc, vc, icb, fcb), st0, need_ctx)
    _, hxf = mlstm_scan(qx, kx, vx, ixf, fxf, st_f, True)
    _, hxb = mlstm_scan(*rev(qx, kx, vx, ixb, fxb), st_b, True)
    h_x = hxf + jnp.flip(hxb, axis=2)
    h_c = hcf + jnp.flip(hcb, axis=2) if need_ctx else None
    return h_c, h_x


def mlstm_out(h, o, w):
    B, H, N, dv = h.shape
    hn = rmsnorm(jnp.swapaxes(h, 1, 2), w.reshape(H, dv)).reshape(B, N, H * dv)
    return (hn * jax.nn.sigmoid(o.astype(jnp.float32))).astype(o.dtype)


def mla_heads(cq, ckv, kpe, q_norm, kv_norm, w_uq, w_ukv, row, col):
    B, N, _ = cq.shape
    q = (rmsnorm(cq, q_norm) @ w_uq).reshape(B, N, B_HEADS, B_NOPE + B_ROPE)
    kv = (rmsnorm(ckv, kv_norm) @ w_ukv).reshape(B, N, B_HEADS, B_NOPE + B_VDIM)
    q_nope, q_pe = q[..., :B_NOPE], q[..., B_NOPE:]
    k_nope, v = kv[..., :B_NOPE], kv[..., B_NOPE:]
    k_pe = kpe[:, :, None, :]
    if row is not None:
        q_pe = axial_rope(q_pe, row, col)
        k_pe = axial_rope(k_pe, row, col)
    q = jnp.concatenate([q_nope, q_pe], axis=-1)
    k = jnp.concatenate([k_nope, jnp.broadcast_to(k_pe, (B, N, B_HEADS, B_ROPE))], axis=-1)
    return q, k, v


def even_mixer(hc, hx, w_in, gate_bias, head_norm, q_norm, kv_norm, w_uq, w_ukv, w_out, row, col, need_ctx):
    B, S, _ = hx.shape
    L = hc.shape[1]
    pc = split_cols(hc @ w_in, A_IN_SIZES)
    px = split_cols(hx @ w_in, A_IN_SIZES)
    mc = mlstm_prep(pc[0], pc[1], pc[2], pc[4], gate_bias)
    mx = mlstm_prep(px[0], px[1], px[2], px[4], gate_bias)
    h_c, h_x = bidir_mlstm(mc, mx, need_ctx)
    a_x = mlstm_out(h_x, px[3], head_norm)
    qc, kc, vc = mla_heads(pc[5], pc[6], pc[7], q_norm, kv_norm, w_uq, w_ukv, None, None)
    qx, kx, vx = mla_heads(px[5], px[6], px[7], q_norm, kv_norm, w_uq, w_ukv, row, col)
    k_all = jnp.concatenate([kc, kx], axis=1)
    v_all = jnp.concatenate([vc, vx], axis=1)
    b_x = over_query_blocks(lambda qi: softmax_attn(qi, k_all, v_all), qx).reshape(B, S, -1)
    y_x = jnp.concatenate([a_x, b_x], axis=-1) @ w_out
    if not need_ctx:
        return None, y_x
    a_c = mlstm_out(h_c, pc[3], head_norm)
    b_c = softmax_attn(qc, kc, vc).reshape(B, L, -1)
    y_c = jnp.concatenate([a_c, b_c], axis=-1) @ w_out
    return y_c, y_x


def odd_mixer(hc, hx, w_in, rpb, lam_p, subln, w_out, row, col, lam_init, need_ctx):
    B, S, _ = hx.shape
    L = hc.shape[1]
    pc = split_cols(hc @ w_in, B_IN_SIZES)
    px = split_cols(hx @ w_in, B_IN_SIZES)
    nq_x, nk_x, nv_x = (to_heads(a, C_HEADS, C_DIM) for a in px[:3])
    nq_c, nk_c, nv_c = (to_heads(a, C_HEADS, C_DIM) for a in pc[:3])
    na_x = neighbourhood_attention(nq_x, nk_x, nv_x, nk_c, nv_c, rpb).reshape(B, S, -1)
    lp = lam_p.astype(jnp.float32)
    lam = jnp.exp(jnp.sum(lp[0] * lp[1])) - jnp.exp(jnp.sum(lp[2] * lp[3])) + lam_init
    dq_x = axial_rope(to_heads(px[3], 2 * D_HEADS, D_DIM), row, col).reshape(B, S, D_HEADS, 2, D_DIM)
    dk_x = axial_rope(to_heads(px[4], 2 * D_HEADS, D_DIM), row, col).reshape(B, S, D_HEADS, 2, D_DIM)
    dv_x = to_heads(px[5], D_HEADS, D_VDIM)
    dq_c = to_heads(pc[3], 2 * D_HEADS, D_DIM).reshape(B, L, D_HEADS, 2, D_DIM)
    dk_c = to_heads(pc[4], 2 * D_HEADS, D_DIM).reshape(B, L, D_HEADS, 2, D_DIM)
    dv_c = to_heads(pc[5], D_HEADS, D_VDIM)
    dk_all = jnp.concatenate([dk_c, dk_x], axis=1)
    dv_all = jnp.concatenate([dv_c, dv_x], axis=1)
    df_x = over_query_blocks(lambda qi: diff_attn(qi, dk_all, dv_all, lam), dq_x)
    y_x = jnp.concatenate([na_x, diff_post(df_x, subln, lam_init)], axis=-1) @ w_out
    if not need_ctx:
        return None, y_x
    na_c = softmax_attn(nq_c, nk_c, nv_c).reshape(B, L, -1)
    df_c = diff_post(diff_attn(dq_c, dk_c, dv_c, lam), subln, lam_init)
    y_c = jnp.concatenate([na_c, df_c], axis=-1) @ w_out
    return y_c, y_x


def moe_ffn(h, router_w, router_bias, w1, w3, w2):
    T, D = h.shape
    E = w1.shape[0]
    epg = E // N_GROUPS
    scores = jax.nn.sigmoid(jnp.dot(h, router_w, preferred_element_type=jnp.float32))
    sel = (scores + router_bias.astype(jnp.float32)).reshape(T, N_GROUPS, epg)
    group_score = jnp.sum(lax.top_k(sel, 2)[0], axis=-1)
    g_idx = lax.top_k(group_score, 1)[1]
    in_group = jnp.take_along_axis(sel, g_idx[:, :, None], axis=1)[:, 0]
    expert = g_idx * epg + lax.top_k(in_group, TOP_K)[1]
    gate = jnp.take_along_axis(scores, expert, axis=1)
    gate = gate / jnp.sum(gate, axis=-1, keepdims=True)
    n_assign = T * TOP_K
    e_flat = expert.reshape(-1)
    tok = jnp.repeat(jnp.arange(T, dtype=jnp.int32), TOP_K)
    order = jnp.argsort(e_flat)
    e_s, tok_s, w_s = e_flat[order], tok[order], gate.reshape(-1)[order]
    counts = jnp.bincount(e_flat, length=E)
    starts = jnp.cumsum(counts) - counts
    padded = (counts + MOE_BLOCK - 1) // MOE_BLOCK * MOE_BLOCK
    pends = jnp.cumsum(padded)
    dest = (pends - padded)[e_s] + jnp.arange(n_assign) - starts[e_s]
    n_rows = ((n_assign + MOE_BLOCK - 1) // MOE_BLOCK + E) * MOE_BLOCK
    buf_tok = jnp.full((n_rows,), T, jnp.int32).at[dest].set(tok_s)
    buf_w = jnp.zeros((n_rows,), jnp.float32).at[dest].set(w_s)
    n_blk = n_rows // MOE_BLOCK
    blk_e = jnp.minimum(jnp.searchsorted(pends, jnp.arange(n_blk) * MOE_BLOCK, side='right'), E - 1)
    h_pad = jnp.concatenate([h, jnp.zeros((1, D), h.dtype)], axis=0)
    xb = h_pad[buf_tok].reshape(n_blk, MOE_BLOCK, D)

    def expert_block(args):
        xi, e = args
        return (jax.nn.silu(xi @ w1[e]) * (xi @ w3[e])) @ w2[e]

    yb = lax.map(expert_block, (xb, blk_e)).reshape(n_rows, D)
    out = jnp.zeros((T + 1, D), h.dtype).at[buf_tok].add(yb * buf_w[:, None].astype(h.dtype))
    return out[:T]


def setup_inputs(seed: int = 0) -> dict:
    key = jax.random.key(seed)
    ks = iter(jax.random.split(key, 32))
    f32 = jnp.float32

    def nrm(shape, s):
        return jax.random.normal(next(ks), shape, f32) * s

    ne, no = (DEPTH + 1) // 2, DEPTH // 2
    a_in = sum(A_IN_SIZES)
    b_in = sum(B_IN_SIZES)
    gate_base = jnp.repeat(jnp.array([0.0, F_GATE_BIAS, 0.0, F_GATE_BIAS], f32), A_HEADS)
    return {
        'x': nrm((BATCH, SEQ, D_MODEL), 1.0),
        'c': nrm((BATCH, D_MODEL), 1.0),
        'ctx': nrm((BATCH, CTX_LEN, D_MODEL), 1.0),
        'c_ctx': nrm((D_MODEL,), 1.0),
        'ada_down': nrm((DEPTH, D_MODEL, ADA_RANK), D_MODEL ** -0.5),
        'ada_up': nrm((DEPTH, ADA_RANK, N_MOD * D_MODEL), 0.5 * ADA_RANK ** -0.5),
        'ada_bias': nrm((DEPTH, N_MOD * D_MODEL), 0.02),
        'norm_mix': 1.0 + nrm((DEPTH, D_MODEL), 0.02),
        'norm_ffn': 1.0 + nrm((DEPTH, D_MODEL), 0.02),
        'norm_final': 1.0 + nrm((D_MODEL,), 0.02),
        'a_w_in': nrm((ne, D_MODEL, a_in), D_MODEL ** -0.5),
        'a_gate_bias': gate_base + nrm((ne, 4 * A_HEADS), 0.1),
        'a_head_norm': 1.0 + nrm((ne, A_HEADS * A_DV), 0.02),
        'a_q_norm': 1.0 + nrm((ne, B_Q_RANK), 0.02),
        'a_kv_norm': 1.0 + nrm((ne, B_KV_RANK), 0.02),
        'a_w_uq': nrm((ne, B_Q_RANK, B_HEADS * (B_NOPE + B_ROPE)), B_Q_RANK ** -0.5),
        'a_w_ukv': nrm((ne, B_KV_RANK, B_HEADS * (B_NOPE + B_VDIM)), B_KV_RANK ** -0.5),
        'a_w_out': nrm((ne, D_MODEL, D_MODEL), D_MODEL ** -0.5),
        'b_w_in': nrm((no, D_MODEL, b_in), D_MODEL ** -0.5),
        'b_rpb': nrm((no, C_HEADS, 2 * C_WIN_R - 1, 2 * C_WIN_C - 1), 0.1),
        'b_lambda': nrm((no, 4, D_DIM), 0.1),
        'b_subln': 1.0 + nrm((no, D_VDIM), 0.02),
        'b_w_out': nrm((no, D_MODEL, D_MODEL), D_MODEL ** -0.5),
        'router_w': nrm((D_MODEL, N_EXPERTS), D_MODEL ** -0.5),
        'router_bias': nrm((N_EXPERTS,), 0.01),
        'moe_w1': nrm((DEPTH, N_EXPERTS, D_MODEL, D_EXPERT), D_MODEL ** -0.5),
        'moe_w3': nrm((DEPTH, N_EXPERTS, D_MODEL, D_EXPERT), D_MODEL ** -0.5),
        'moe_w2': nrm((DEPTH, N_EXPERTS, D_EXPERT, D_MODEL), D_EXPERT ** -0.5),
    }


def reference(x, c, ctx, c_ctx, ada_down, ada_up, ada_bias, norm_mix, norm_ffn, norm_final,
              a_w_in, a_gate_bias, a_head_norm, a_q_norm, a_kv_norm, a_w_uq, a_w_ukv, a_w_out,
              b_w_in, b_rpb, b_lambda, b_subln, b_w_out,
              router_w, router_bias, moe_w1, moe_w3, moe_w2):
    B, S, D = x.shape
    L = ctx.shape[1]
    t = jnp.arange(S, dtype=jnp.int32)
    row, col = t // GRID_W, t % GRID_W
    xc = ctx
    for l in range(DEPTH):
        need_ctx = l < DEPTH - 1
        j = l // 2
        m_x = [m[:, None, :] for m in ada_mod(c, ada_down[l], ada_up[l], ada_bias[l])]
        m_c = ada_mod(c_ctx, ada_down[l], ada_up[l], ada_bias[l])
        hx = modulate(x, norm_mix[l], m_x[0], m_x[1])
        hc = modulate(xc, norm_mix[l], m_c[0], m_c[1])
        if l % 2 == 0:
            y_c, y_x = even_mixer(hc, hx, a_w_in[j], a_gate_bias[j], a_head_norm[j], a_q_norm[j], a_kv_norm[j],
                                  a_w_uq[j], a_w_ukv[j], a_w_out[j], row, col, need_ctx)
        else:
            lam_init = 0.8 - 0.6 * math.exp(-0.3 * l)
            y_c, y_x = odd_mixer(hc, hx, b_w_in[j], b_rpb[j], b_lambda[j], b_subln[j], b_w_out[j],
                                 row, col, lam_init, need_ctx)
        x = x + m_x[2] * y_x
        hx = modulate(x, norm_ffn[l], m_x[3], m_x[4])
        if need_ctx:
            xc = xc + m_c[2] * y_c
            hc = modulate(xc, norm_ffn[l], m_c[3], m_c[4])
            toks = jnp.concatenate([hc.reshape(-1, D), hx.reshape(-1, D)], axis=0)
            f = moe_ffn(toks, router_w, router_bias, moe_w1[l], moe_w3[l], moe_w2[l])
            xc = xc + m_c[5] * f[:B * L].reshape(B, L, D)
            f_x = f[B * L:].reshape(B, S, D)
        else:
            f_x = moe_ffn(hx.reshape(-1, D), router_w, router_bias, moe_w1[l], moe_w3[l], moe_w2[l]).reshape(B, S, D)
        x = x + m_x[5] * f_x
    return rmsnorm(x, norm_final)
```

```python
import functools
import math

import jax
import jax.numpy as jnp
from jax import lax
from jax.experimental import pallas as pl
from jax.experimental.pallas import tpu as pltpu

F32 = jnp.float32
BF16 = jnp.bfloat16

GRID_W = 64
EPS = 1e-6
NEG_INF = -1e30
ROPE_BASE = 10000.0
N_MOD = 6
A_DK = 256
A_DV = 512
B_VDIM = 128
B_NOPE = 128
B_ROPE = 64
B_Q_RANK = 1024
B_KV_RANK = 512
C_DIM = 128
C_WIN_R = 8
C_WIN_C = 16
D_DIM = 128
D_VDIM = 256
N_GROUPS = 4
TOP_K = 2

LANES = 128
VMEM_LIMIT_BYTES = 56 * 1024 * 1024
MLA_HEAD_PAD = 256

A_CHUNK_K = 256
MOE_ROWS = 512
NAT_ROWS = 8


def _pick(n, cands):
    for c in cands:
        if n % c == 0:
            return c
    return n


def _params(sem):
    return pltpu.CompilerParams(dimension_semantics=sem, vmem_limit_bytes=VMEM_LIMIT_BYTES)


def _mm_body(*refs, prologue, has_bias, has_tab, rope_d, has_resid, n_split, tm):
    it = iter(refs)
    a_ref = next(it)
    b_ref = next(it)
    nw_ref = next(it) if prologue == "rmsnorm" else None
    bias_ref = next(it) if has_bias else None
    tab_ref = next(it) if has_tab else None
    if rope_d:
        cos_ref, sa_ref, sb_ref = next(it), next(it), next(it)
    if has_resid:
        res_ref, gate_ref = next(it), next(it)
    o_ref = next(it)

    a = a_ref[...]
    if prologue == "silu":
        af = a.astype(F32)
        a = af * jax.nn.sigmoid(af)
    elif prologue == "rmsnorm":
        af = a.astype(F32)
        a = af * lax.rsqrt(jnp.mean(af * af, axis=-1, keepdims=True) + EPS) * nw_ref[...]
    y = jnp.dot(a.astype(BF16), b_ref[...].astype(BF16), preferred_element_type=F32)
    if has_bias:
        y = y + bias_ref[...]
    if has_tab:
        y = y + tab_ref[...]
    if rope_d:
        w = y.shape[1]
        y = y * cos_ref[...] + pltpu.roll(y, w - rope_d, 1) * sa_ref[...] + pltpu.roll(y, rope_d, 1) * sb_ref[...]
    if has_resid:
        rows = pl.program_id(0) * tm + lax.broadcasted_iota(jnp.int32, (tm, 1), 0)
        g = jnp.where(rows < n_split, gate_ref[0:1, :], gate_ref[1:2, :])
        y = res_ref[...] + g * y
    o_ref[...] = y.astype(o_ref.dtype)


def _mm(a, b, *, n_cols, tm, tn, out_dtype, b_lead=None, b_col0=0, a_k=None, a_col0=0, prologue=None,
        norm_w=None, bias=None, tab=None, rope=None, rope_d=0, resid=None, gate=None, n_split=0, name="mm"):
    m = a.shape[0]
    k = a_k if a_k is not None else a.shape[1]
    assert m % tm == 0 and n_cols % tn == 0 and b_col0 % tn == 0 and a_col0 % k == 0
    acb, bcb = a_col0 // k, b_col0 // tn
    in_specs = [pl.BlockSpec((tm, k), lambda i, j: (i, acb))]
    if b_lead is None:
        in_specs.append(pl.BlockSpec((k, tn), lambda i, j: (0, j + bcb)))
    else:
        in_specs.append(pl.BlockSpec((None, k, tn), lambda i, j: (b_lead, 0, j + bcb)))
    args = [a, b]
    if prologue == "rmsnorm":
        in_specs.append(pl.BlockSpec((1, k), lambda i, j: (0, 0)))
        args.append(norm_w.reshape(1, k).astype(F32))
    if bias is not None:
        in_specs.append(pl.BlockSpec((1, tn), lambda i, j: (0, j)))
        args.append(bias.reshape(1, n_cols).astype(F32))
    if tab is not None:
        in_specs.append(pl.BlockSpec((tm, tn), lambda i, j: (i, 0)))
        args.append(tab)
    if rope is not None:
        for t in rope:
            in_specs.append(pl.BlockSpec((tm, tn), lambda i, j: (i, 0)))
            args.append(t)
    if resid is not None:
        in_specs.append(pl.BlockSpec((tm, tn), lambda i, j: (i, j)))
        in_specs.append(pl.BlockSpec((2, tn), lambda i, j: (0, j)))
        args += [resid, gate]
    body = functools.partial(_mm_body, prologue=prologue, has_bias=bias is not None, has_tab=tab is not None,
                             rope_d=rope_d if rope is not None else 0, has_resid=resid is not None,
                             n_split=n_split, tm=tm)
    return pl.pallas_call(
        body,
        out_shape=jax.ShapeDtypeStruct((m, n_cols), out_dtype),
        grid=(m // tm, n_cols // tn),
        in_specs=in_specs,
        out_specs=pl.BlockSpec((tm, tn), lambda i, j: (i, j)),
        compiler_params=_params(("parallel", "arbitrary")),
        name=name,
    )(*args)


def _norm_mod_body(x_ref, w_ref, sh_ref, sc_ref, o_ref, *, n_split, tm):
    x = x_ref[...]
    y = x * lax.rsqrt(jnp.mean(x * x, axis=-1, keepdims=True) + EPS) * w_ref[...]
    rows = pl.program_id(0) * tm + lax.broadcasted_iota(jnp.int32, (tm, 1), 0)
    lat = rows < n_split
    sc = jnp.where(lat, sc_ref[0:1, :], sc_ref[1:2, :])
    sh = jnp.where(lat, sh_ref[0:1, :], sh_ref[1:2, :])
    o_ref[...] = (y * (1.0 + sc) + sh).astype(o_ref.dtype)


def _norm_mod(x, w, shift, scale, n_split, out_dtype):
    n, d = x.shape
    tm = _pick(n, (256, 128, 64, 32, 16, 8))
    return pl.pallas_call(
        functools.partial(_norm_mod_body, n_split=n_split, tm=tm),
        out_shape=jax.ShapeDtypeStruct((n, d), out_dtype),
        grid=(n // tm,),
        in_specs=[pl.BlockSpec((tm, d), lambda i: (i, 0)),
                  pl.BlockSpec((1, d), lambda i: (0, 0)),
                  pl.BlockSpec((2, d), lambda i: (0, 0)),
                  pl.BlockSpec((2, d), lambda i: (0, 0))],
        out_specs=pl.BlockSpec((tm, d), lambda i: (i, 0)),
        compiler_params=_params(("parallel",)),
        name="norm_mod",
    )(x, w.reshape(1, d), shift, scale)


def _rmsnorm_body(x_ref, w_ref, o_ref):
    x = x_ref[...]
    o_ref[...] = x * lax.rsqrt(jnp.mean(x * x, axis=-1, keepdims=True) + EPS) * w_ref[...]


def _rmsnorm_rows(x, w, n_rows):
    d = x.shape[1]
    tm = _pick(n_rows, (256, 128, 64, 32, 16, 8))
    return pl.pallas_call(
        _rmsnorm_body,
        out_shape=jax.ShapeDtypeStruct((n_rows, d), F32),
        grid=(n_rows // tm,),
        in_specs=[pl.BlockSpec((tm, d), lambda i: (i, 0)), pl.BlockSpec((1, d), lambda i: (0, 0))],
        out_specs=pl.BlockSpec((tm, d), lambda i: (i, 0)),
        compiler_params=_params(("parallel",)),
        name="final_norm",
    )(x, w.reshape(1, d))


def _gated_add_body(x_ref, f_ref, g_ref, o_ref, *, n_split, tm):
    rows = pl.program_id(0) * tm + lax.broadcasted_iota(jnp.int32, (tm, 1), 0)
    g = jnp.where(rows < n_split, g_ref[0:1, :], g_ref[1:2, :])
    o_ref[...] = x_ref[...] + g * f_ref[...]


def _gated_add(x, f, gate, n_split):
    n, d = x.shape
    tm = _pick(n, (256, 128, 64, 32, 16, 8))
    return pl.pallas_call(
        functools.partial(_gated_add_body, n_split=n_split, tm=tm),
        out_shape=jax.ShapeDtypeStruct((n, d), F32),
        grid=(n // tm,),
        in_specs=[pl.BlockSpec((tm, d), lambda i: (i, 0)),
                  pl.BlockSpec((tm, d), lambda i: (i, 0)),
                  pl.BlockSpec((2, d), lambda i: (0, 0))],
        out_specs=pl.BlockSpec((tm, d), lambda i: (i, 0)),
        compiler_params=_params(("parallel",)),
        name="gated_add",
    )(x, f, gate)


def _flash_body(q_ref, k_ref, v_ref, o_ref, m_sc, l_sc, acc_sc, *, scale, nk):
    j = pl.program_id(2)

    @pl.when(j == 0)
    def _():
        m_sc[...] = jnp.full(m_sc.shape, -jnp.inf, F32)
        l_sc[...] = jnp.zeros(l_sc.shape, F32)
        acc_sc[...] = jnp.zeros(acc_sc.shape, F32)

    s = lax.dot_general(q_ref[...], k_ref[...], (((1,), (1,)), ((), ())), preferred_element_type=F32) * scale
    m_prev = m_sc[...]
    m_new = jnp.maximum(m_prev, jnp.max(s, axis=-1, keepdims=True))
    alpha = jnp.exp(m_prev - m_new)
    p = jnp.exp(s - m_new)
    l_sc[...] = alpha * l_sc[...] + jnp.sum(p, axis=-1, keepdims=True)
    acc_sc[...] = alpha * acc_sc[...] + jnp.dot(p.astype(BF16), v_ref[...], preferred_element_type=F32)
    m_sc[...] = m_new

    @pl.when(j == nk - 1)
    def _():
        o_ref[...] = (acc_sc[...] / l_sc[...]).astype(o_ref.dtype)


def _flash(q, k, v, *, heads, dk, dv, q_col0, k_col0, v_col0, v_share, q_row0, q_rows, tq, k_row0, k_rows, tk,
           scale, out_dtype, name):
    assert q_rows % tq == 0 and k_rows % tk == 0 and q_row0 % tq == 0 and k_row0 % tk == 0
    assert q_col0 % dk == 0 and k_col0 % dk == 0 and v_col0 % dv == 0
    qrb, krb, qcb, kcb, vcb = q_row0 // tq, k_row0 // tk, q_col0 // dk, k_col0 // dk, v_col0 // dv
    nq, nk = q_rows // tq, k_rows // tk
    return pl.pallas_call(
        functools.partial(_flash_body, scale=scale, nk=nk),
        out_shape=jax.ShapeDtypeStruct((q_rows, heads * dv), out_dtype),
        grid=(heads, nq, nk),
        in_specs=[pl.BlockSpec((tq, dk), lambda h, i, j: (i + qrb, h + qcb)),
                  pl.BlockSpec((tk, dk), lambda h, i, j: (j + krb, h + kcb)),
                  pl.BlockSpec((tk, dv), lambda h, i, j: (j + krb, h // v_share + vcb))],
        out_specs=pl.BlockSpec((tq, dv), lambda h, i, j: (i, h)),
        scratch_shapes=[pltpu.VMEM((tq, 1), F32), pltpu.VMEM((tq, 1), F32), pltpu.VMEM((tq, dv), F32)],
        compiler_params=_params(("parallel", "parallel", "arbitrary")),
        name=name,
    )(q, k, v)


def _diff_post_body(o_ref, lam_ref, w_ref, out_ref, *, lam_init, dv):
    lp = lam_ref[...]
    lam = (jnp.exp(jnp.sum(lp[0:1, :] * lp[1:2, :], axis=-1, keepdims=True))
           - jnp.exp(jnp.sum(lp[2:3, :] * lp[3:4, :], axis=-1, keepdims=True)) + lam_init)
    o = o_ref[...]
    d = o[:, :dv] - lam * o[:, dv:]
    y = d * lax.rsqrt(jnp.mean(d * d, axis=-1, keepdims=True) + EPS) * w_ref[...]
    out_ref[...] = (y * (1.0 - lam_init)).astype(out_ref.dtype)


def _diff_post(o, lam_p, subln, lam_init, heads, dv):
    n = o.shape[0]
    tm = _pick(n, (256, 128, 64, 32, 16, 8))
    return pl.pallas_call(
        functools.partial(_diff_post_body, lam_init=lam_init, dv=dv),
        out_shape=jax.ShapeDtypeStruct((n, heads * dv), BF16),
        grid=(n // tm, heads),
        in_specs=[pl.BlockSpec((tm, 2 * dv), lambda i, h: (i, h)),
                  pl.BlockSpec(lam_p.shape, lambda i, h: (0, 0)),
                  pl.BlockSpec((1, dv), lambda i, h: (0, 0))],
        out_specs=pl.BlockSpec((tm, dv), lambda i, h: (i, h)),
        compiler_params=_params(("parallel", "parallel")),
        name="diff_post",
    )(o, lam_p.astype(F32), subln.reshape(1, dv).astype(F32))


def _natten_body(q_ref, k_ref, v_ref, kc_ref, vc_ref, tab_ref, o_ref, *, scale, rows, kr, wg, rb):
    blk = pl.program_id(1)
    kc = kc_ref[...]
    vc = vc_ref[...]

    def one_row(rr, carry):
        i = blk * rb + rr
        r0 = jnp.clip(i - kr // 2, 0, rows - kr)
        idx0 = r0 - i + (kr - 1)
        q = q_ref[pl.ds(pl.multiple_of(rr * wg, wg), wg), :]
        koff = pl.multiple_of(r0 * wg, wg)
        kw = k_ref[pl.ds(koff, kr * wg), :]
        vw = v_ref[pl.ds(koff, kr * wg), :]
        s_loc = lax.dot_general(q, kw, (((1,), (1,)), ((), ())), preferred_element_type=F32) * scale + tab_ref[idx0]
        s_ctx = lax.dot_general(q, kc, (((1,), (1,)), ((), ())), preferred_element_type=F32) * scale
        m = jnp.maximum(jnp.max(s_loc, axis=-1, keepdims=True), jnp.max(s_ctx, axis=-1, keepdims=True))
        p_loc = jnp.exp(s_loc - m)
        p_ctx = jnp.exp(s_ctx - m)
        l = jnp.sum(p_loc, axis=-1, keepdims=True) + jnp.sum(p_ctx, axis=-1, keepdims=True)
        o = (jnp.dot(p_ctx.astype(BF16), vc, preferred_element_type=F32)
             + jnp.dot(p_loc.astype(BF16), vw, preferred_element_type=F32)) / l
        o_ref[pl.ds(pl.multiple_of(rr * wg, wg), wg), :] = o.astype(o_ref.dtype)
        return carry

    lax.fori_loop(0, rb, one_row, 0)


def _natten(p, tab, *, heads, d, q_col0, k_col0, v_col0, s_len, l_len, wg, kr):
    rows = s_len // wg
    rb = _pick(rows, (NAT_ROWS, 4, 2, 1))
    qcb, kcb, vcb = q_col0 // d, k_col0 // d, v_col0 // d
    crb = s_len // l_len
    assert s_len % l_len == 0
    return pl.pallas_call(
        functools.partial(_natten_body, scale=d ** -0.5, rows=rows, kr=kr, wg=wg, rb=rb),
        out_shape=jax.ShapeDtypeStruct((s_len, heads * d), BF16),
        grid=(heads, rows // rb),
        in_specs=[pl.BlockSpec((rb * wg, d), lambda h, i: (i, h + qcb)),
                  pl.BlockSpec((s_len, d), lambda h, i: (0, h + kcb)),
                  pl.BlockSpec((s_len, d), lambda h, i: (0, h + vcb)),
                  pl.BlockSpec((l_len, d), lambda h, i: (crb, h + kcb)),
                  pl.BlockSpec((l_len, d), lambda h, i: (crb, h + vcb)),
                  pl.BlockSpec((None, kr, wg, kr * wg), lambda h, i: (h, 0, 0, 0))],
        out_specs=pl.BlockSpec((rb * wg, d), lambda h, i: (i, h)),
        compiler_params=_params(("parallel", "arbitrary")),
        name="natten",
    )(p, p, p, p, p, tab)


def _natten_table(rpb, wg, kr):
    cols = jnp.arange(wg)
    c0 = jnp.clip(cols - C_WIN_C // 2, 0, wg - C_WIN_C)
    col_ok = (cols[None, :] >= c0[:, None]) & (cols[None, :] < c0[:, None] + C_WIN_C)
    dc_idx = jnp.clip(cols[None, :] - cols[:, None] + C_WIN_C - 1, 0, 2 * C_WIN_C - 2)
    bias_c = rpb.astype(F32)[:, :, dc_idx]
    dr = (jnp.arange(kr)[:, None] - (kr - 1)) + jnp.arange(kr)[None, :] + C_WIN_R - 1
    t = bias_c[:, dr]
    t = jnp.where(col_ok[None, None, None], t, NEG_INF)
    t = t.transpose(0, 1, 3, 2, 4)
    return t.reshape(t.shape[0], kr, wg, kr * wg)


def _log_sigmoid(x):
    return jnp.minimum(x, 0.0) - jnp.log1p(jnp.exp(-jnp.abs(x)))


def _mlstm_body(q_ref, k_ref, v_ref, gr_ref, gc_ref, br_ref, bc_ref, o_ref, ct_sc, n_sc, m_sc, *, t, n_heads, qscale):
    dh = pl.program_id(0)
    c = pl.program_id(1)
    backward = dh >= n_heads

    @pl.when(c == 0)
    def _():
        ct_sc[...] = jnp.zeros(ct_sc.shape, F32)
        n_sc[...] = jnp.zeros(n_sc.shape, F32)
        m_sc[...] = jnp.zeros(m_sc.shape, F32)

    gr = gr_ref[...] + br_ref[...]
    gc = gc_ref[...] + bc_ref[...]
    ig_r, lf_r = gr[0:1, :], _log_sigmoid(gr[1:2, :])
    ig_c, lf_c = gc[:, 0:1], _log_sigmoid(gc[:, 1:2])

    jj = lax.broadcasted_iota(jnp.int32, (t, t), 0)
    ss = lax.broadcasted_iota(jnp.int32, (t, t), 1)
    sgn = jnp.where(backward, -1, 1)
    incl = (ss - jj) * sgn <= 0
    incl_t = (jj - ss) * sgn <= 0
    b_c = jnp.sum(jnp.where(incl, lf_r, 0.0), axis=1, keepdims=True)
    b_r = jnp.sum(jnp.where(incl_t, lf_c, 0.0), axis=0, keepdims=True)
    total = jnp.sum(lf_r, axis=1, keepdims=True)

    m_prev = m_sc[...]
    w_end_r = total - b_r + ig_r
    w_end_c = total - b_c + ig_c
    m_new = jnp.maximum(total + m_prev, jnp.max(w_end_r, axis=1, keepdims=True))
    decay = jnp.exp(total + m_prev - m_new)
    ws_c = jnp.exp(w_end_c - m_new)

    q = q_ref[...] * qscale
    k = k_ref[...]
    v = v_ref[...]
    ct = ct_sc[...]
    n_row = n_sc[...]

    logw = jnp.where(incl, b_c - b_r + ig_r, NEG_INF)
    inter = b_c + m_prev
    m_row = jnp.maximum(inter, jnp.max(logw, axis=1, keepdims=True))
    qk = lax.dot_general(q, k, (((1,), (1,)), ((), ())), preferred_element_type=F32)
    amat = jnp.exp(logw - m_row) * qk
    e_inter = jnp.exp(inter - m_row)
    num = (jnp.dot(amat.astype(BF16), v, preferred_element_type=F32)
           + e_inter * jnp.dot(q, ct.astype(BF16), preferred_element_type=F32))
    qf = q.astype(F32)
    den = jnp.sum(amat, axis=1, keepdims=True) + e_inter * jnp.sum(qf * n_row, axis=1, keepdims=True)
    o_ref[...] = num / jnp.maximum(jnp.abs(den), jnp.exp(-m_row))

    kf = k.astype(F32)
    vws = (v.astype(F32) * ws_c).astype(BF16)
    ct_sc[...] = decay * ct + lax.dot_general(k, vws, (((0,), (0,)), ((), ())), preferred_element_type=F32)
    n_sc[...] = decay * n_row + jnp.sum(kf * ws_c, axis=0, keepdims=True)
    m_sc[...] = m_new


def _mlstm(qkv, gates, gate_bias, *, n_heads, dk, dv, s_len, l_len, t):
    n = s_len + l_len
    nx, nc = s_len // t, l_len // t
    assert s_len % t == 0 and l_len % t == 0
    g4 = gates.reshape(n, 2, 2, n_heads)
    g_sel = g4.transpose(1, 3, 0, 2).reshape(2 * n_heads, n, 2)
    g_col = g_sel
    g_row = g_sel.reshape(2 * n_heads, n // t, t, 2).transpose(0, 1, 3, 2)
    b4 = gate_bias.astype(F32).reshape(2, 2, n_heads).transpose(0, 2, 1).reshape(2 * n_heads, 2)
    b_row = b4.reshape(2 * n_heads, 2, 1)
    b_col = b4.reshape(2 * n_heads, 1, 2)

    def blk(dh, c):
        fwd = jnp.where(c < nc, nx + c, c - nc)
        bwd = jnp.where(c < nc, nx + nc - 1 - c, nx - 1 - (c - nc))
        return jnp.where(dh >= n_heads, bwd, fwd)

    kcb, vcb = n_heads, 2 * n_heads * dk // dv
    return pl.pallas_call(
        functools.partial(_mlstm_body, t=t, n_heads=n_heads, qscale=dk ** -0.5),
        out_shape=jax.ShapeDtypeStruct((2, n, n_heads * dv), F32),
        grid=(2 * n_heads, nx + nc),
        in_specs=[pl.BlockSpec((t, dk), lambda dh, c: (blk(dh, c), dh % n_heads)),
                  pl.BlockSpec((t, dk), lambda dh, c: (blk(dh, c), kcb + dh % n_heads)),
                  pl.BlockSpec((t, dv), lambda dh, c: (blk(dh, c), vcb + dh % n_heads)),
                  pl.BlockSpec((None, None, 2, t), lambda dh, c: (dh, blk(dh, c), 0, 0)),
                  pl.BlockSpec((None, t, 2), lambda dh, c: (dh, blk(dh, c), 0)),
                  pl.BlockSpec((None, 2, 1), lambda dh, c: (dh, 0, 0)),
                  pl.BlockSpec((None, 1, 2), lambda dh, c: (dh, 0, 0))],
        out_specs=pl.BlockSpec((None, t, dv), lambda dh, c: (dh // n_heads, blk(dh, c), dh % n_heads)),
        scratch_shapes=[pltpu.VMEM((dk, dv), F32), pltpu.VMEM((1, dk), F32), pltpu.VMEM((1, 1), F32)],
        compiler_params=_params(("parallel", "arbitrary")),
        name="mlstm",
    )(qkv, qkv, qkv, g_row, g_col, b_row, b_col)


def _mlstm_out_body(hf_ref, hb_ref, o_ref, w_ref, out_ref):
    h = hf_ref[...] + hb_ref[...]
    hn = h * lax.rsqrt(jnp.mean(h * h, axis=-1, keepdims=True) + EPS) * w_ref[...]
    out_ref[...] = (hn * jax.nn.sigmoid(o_ref[...])).astype(out_ref.dtype)


def _mlstm_out(hs, o, w, n_heads, dv):
    n = o.shape[0]
    tm = _pick(n, (256, 128, 64, 32, 16, 8))
    return pl.pallas_call(
        _mlstm_out_body,
        out_shape=jax.ShapeDtypeStruct((n, n_heads * dv), BF16),
        grid=(n // tm, n_heads),
        in_specs=[pl.BlockSpec((None, tm, dv), lambda i, h: (0, i, h)),
                  pl.BlockSpec((None, tm, dv), lambda i, h: (1, i, h)),
                  pl.BlockSpec((tm, dv), lambda i, h: (i, h)),
                  pl.BlockSpec((1, dv), lambda i, h: (0, h))],
        out_specs=pl.BlockSpec((tm, dv), lambda i, h: (i, h)),
        compiler_params=_params(("parallel", "parallel")),
        name="mlstm_out",
    )(hs, hs, o, w.reshape(1, n_heads * dv).astype(F32))


def _router_body(h_ref, wt_ref, b_ref, e_ref, g_ref, *, n_exp, n_groups):
    logits = lax.dot_general(wt_ref[...], h_ref[...], (((1,), (1,)), ((), ())), preferred_element_type=F32,
                             precision=lax.Precision.HIGHEST)
    scores = jax.nn.sigmoid(logits)
    sel = scores + b_ref[...]
    epg = n_exp // n_groups
    sc = [scores[e:e + 1, :] for e in range(n_exp)]
    sl = [sel[e:e + 1, :] for e in range(n_exp)]

    def top2_sum(vals):
        best = jnp.maximum(vals[0], vals[1])
        second = jnp.minimum(vals[0], vals[1])
        for x in vals[2:]:
            second = jnp.maximum(second, jnp.minimum(best, x))
            best = jnp.maximum(best, x)
        return best + second

    gs = [top2_sum(sl[g * epg:(g + 1) * epg]) for g in range(n_groups)]
    g_best, g_idx = gs[0], jnp.zeros_like(gs[0], dtype=jnp.int32)
    for g in range(1, n_groups):
        better = gs[g] > g_best
        g_best = jnp.where(better, gs[g], g_best)
        g_idx = jnp.where(better, g, g_idx)
    in_sel, in_raw = [], []
    for j in range(epg):
        vs, vr = sl[j], sc[j]
        for g in range(1, n_groups):
            vs = jnp.where(g_idx == g, sl[g * epg + j], vs)
            vr = jnp.where(g_idx == g, sc[g * epg + j], vr)
        in_sel.append(vs)
        in_raw.append(vr)
    v1, i1, r1 = in_sel[0], jnp.zeros_like(g_idx), in_raw[0]
    for j in range(1, epg):
        better = in_sel[j] > v1
        v1 = jnp.where(better, in_sel[j], v1)
        i1 = jnp.where(better, j, i1)
        r1 = jnp.where(better, in_raw[j], r1)
    v2 = jnp.full_like(v1, -jnp.inf)
    i2, r2 = jnp.zeros_like(g_idx), jnp.zeros_like(r1)
    for j in range(epg):
        better = (in_sel[j] > v2) & (i1 != j)
        v2 = jnp.where(better, in_sel[j], v2)
        i2 = jnp.where(better, j, i2)
        r2 = jnp.where(better, in_raw[j], r2)
    tot = r1 + r2
    e_ref[...] = jnp.concatenate([g_idx * epg + i1, g_idx * epg + i2], axis=0)
    g_ref[...] = jnp.concatenate([r1 / tot, r2 / tot], axis=0)


def _router(h, router_w, router_bias):
    n, d = h.shape
    n_exp = router_w.shape[1]
    tm = _pick(n, (256, 128))
    return pl.pallas_call(
        functools.partial(_router_body, n_exp=n_exp, n_groups=N_GROUPS),
        out_shape=(jax.ShapeDtypeStruct((TOP_K, n), jnp.int32), jax.ShapeDtypeStruct((TOP_K, n), F32)),
        grid=(n // tm,),
        in_specs=[pl.BlockSpec((tm, d), lambda i: (i, 0)),
                  pl.BlockSpec((n_exp, d), lambda i: (0, 0)),
                  pl.BlockSpec((n_exp, 1), lambda i: (0, 0))],
        out_specs=(pl.BlockSpec((TOP_K, tm), lambda i: (0, i)), pl.BlockSpec((TOP_K, tm), lambda i: (0, i))),
        compiler_params=_params(("parallel",)),
        name="router",
    )(h, router_w.T.astype(F32), router_bias.reshape(n_exp, 1).astype(F32))


def _experts_body(be_ref, bv_ref, x_ref, w1_ref, w3_ref, w2_ref, rw_ref, o_ref):
    b = pl.program_id(0)
    f = pl.program_id(1)

    @pl.when(f == 0)
    def _():
        o_ref[...] = jnp.zeros(o_ref.shape, F32)

    @pl.when(bv_ref[b] > 0)
    def _():
        x = x_ref[...]
        h1 = jnp.dot(x, w1_ref[...].astype(BF16), preferred_element_type=F32)
        h3 = jnp.dot(x, w3_ref[...].astype(BF16), preferred_element_type=F32)
        act = (h1 * jax.nn.sigmoid(h1) * h3).astype(BF16)
        y = jnp.dot(act, w2_ref[...].astype(BF16), preferred_element_type=F32)
        o_ref[...] += y * rw_ref[...]


def _experts(xb, row_w, blk_e, blk_valid, w1, w3, w2, layer, tb, fc):
    n_rows, d = xb.shape
    d_exp = w1.shape[-1]
    nf = d_exp // fc
    nb = n_rows // tb

    def fsel(b, f, bv):
        return jnp.where(bv[b] > 0, f, nf - 1)

    grid_spec = pltpu.PrefetchScalarGridSpec(
        num_scalar_prefetch=2,
        grid=(nb, nf),
        in_specs=[pl.BlockSpec((tb, d), lambda b, f, be, bv: (b, 0)),
                  pl.BlockSpec((None, None, d, fc), lambda b, f, be, bv: (layer, be[b], 0, fsel(b, f, bv))),
                  pl.BlockSpec((None, None, d, fc), lambda b, f, be, bv: (layer, be[b], 0, fsel(b, f, bv))),
                  pl.BlockSpec((None, None, fc, d), lambda b, f, be, bv: (layer, be[b], fsel(b, f, bv), 0)),
                  pl.BlockSpec((tb, 1), lambda b, f, be, bv: (b, 0))],
        out_specs=pl.BlockSpec((tb, d), lambda b, f, be, bv: (b, 0)),
    )
    return pl.pallas_call(
        _experts_body,
        out_shape=jax.ShapeDtypeStruct((n_rows, d), F32),
        grid_spec=grid_spec,
        compiler_params=_params(("arbitrary", "arbitrary")),
        name="experts",
    )(blk_e, blk_valid, xb, w1, w3, w2, row_w)


def _moe(h, router_w, router_bias, w1, w3, w2, layer):
    t_tok, d = h.shape
    n_exp = router_w.shape[1]
    tb = MOE_ROWS
    expert, gate = _router(h, router_w, router_bias)
    n_assign = t_tok * TOP_K
    e_flat = expert.T.reshape(-1)
    w_flat = gate.T.reshape(-1)
    onehot = (e_flat[:, None] == jnp.arange(n_exp, dtype=jnp.int32)[None, :]).astype(jnp.int32)
    rank = jnp.take_along_axis(jnp.cumsum(onehot, axis=0) - onehot, e_flat[:, None], axis=1)[:, 0]
    counts = jnp.sum(onehot, axis=0)
    padded = (counts + tb - 1) // tb * tb
    pends = jnp.cumsum(padded)
    dest = (pends - padded)[e_flat] + rank
    nb = (n_assign + tb - 1) // tb + n_exp
    n_rows = nb * tb
    tok = jnp.arange(n_assign, dtype=jnp.int32) // TOP_K
    buf_tok = jnp.zeros((n_rows,), jnp.int32).at[dest].set(tok)
    buf_w = jnp.zeros((n_rows,), F32).at[dest].set(w_flat)
    blk_start = jnp.arange(nb, dtype=jnp.int32) * tb
    blk_e = jnp.minimum(jnp.searchsorted(pends, blk_start, side="right"), n_exp - 1).astype(jnp.int32)
    blk_valid = (blk_start < pends[-1]).astype(jnp.int32)
    xb = jnp.take(h, buf_tok, axis=0).astype(BF16)
    fc = _pick(w1.shape[-1], (256, 128))
    yb = _experts(xb, buf_w.reshape(n_rows, 1), blk_e, blk_valid, w1, w3, w2, layer, tb, fc)
    d2 = dest.reshape(t_tok, TOP_K)
    return jnp.take(yb, d2[:, 0], axis=0) + jnp.take(yb, d2[:, 1], axis=0)


def _axial_tables(s_len, l_len, head_w, pe_off, pe_dim, reps):
    t = jnp.arange(s_len, dtype=jnp.int32)
    pos = jnp.stack([t // GRID_W, t % GRID_W], axis=0).astype(F32)
    h = pe_dim // 2
    d = h // 2
    inv = ROPE_BASE ** (-jnp.arange(d, dtype=F32) / d)
    lane = jnp.arange(pe_dim)
    sub, rr = lane // h, lane % h
    ang = pos[sub, :].T * inv[rr % d][None, :]
    first = (rr < d)[None, :]
    cos = jnp.ones((s_len, head_w), F32).at[:, pe_off:pe_off + pe_dim].set(jnp.cos(ang))
    sa = jnp.zeros((s_len, head_w), F32).at[:, pe_off:pe_off + pe_dim].set(jnp.where(first, -jnp.sin(ang), 0.0))
    sb = jnp.zeros((s_len, head_w), F32).at[:, pe_off:pe_off + pe_dim].set(jnp.where(first, 0.0, jnp.sin(ang)))
    cos = jnp.concatenate([cos, jnp.ones((l_len, head_w), F32)], axis=0)
    sa = jnp.concatenate([sa, jnp.zeros((l_len, head_w), F32)], axis=0)
    sb = jnp.concatenate([sb, jnp.zeros((l_len, head_w), F32)], axis=0)
    return tuple(jnp.tile(a, (1, reps)) for a in (cos, sa, sb)), d


def _even_mixer(h, xa, gate, lw, s_len, l_len, tm):
    (w_in, j, gate_bias, head_norm, q_norm, kv_norm, w_uq, w_ukv, w_out, mla_tabs, mla_d) = lw
    n, d = h.shape
    half = d // 2
    a_heads = half // A_DV
    b_heads = half // B_VDIM
    n_qk = 2 * a_heads * A_DK
    n_v = a_heads * A_DV
    qkv = _mm(h, w_in, b_lead=j, n_cols=n_qk + n_v, tm=tm, tn=_pick(n_qk + n_v, (512, 256, 128)), out_dtype=BF16,
              name="a_in_qkv")
    o_gate = _mm(h, w_in, b_lead=j, b_col0=n_qk + n_v, n_cols=n_v, tm=tm, tn=_pick(n_v, (512, 256, 128)),
                 out_dtype=F32, name="a_in_o")
    off = n_qk + 2 * n_v
    n_g = 4 * a_heads
    w_misc = jnp.concatenate([
        w_in[j][:, off + n_g:off + n_g + B_Q_RANK + B_KV_RANK],
        w_in[j][:, off:off + n_g],
        w_in[j][:, off + n_g + B_Q_RANK + B_KV_RANK:]], axis=1)
    n_misc = w_misc.shape[1]
    n_misc_pad = -(-n_misc // 256) * 256
    w_misc = jnp.pad(w_misc, ((0, 0), (0, n_misc_pad - n_misc)))
    misc = _mm(h, w_misc, n_cols=n_misc_pad, tm=tm, tn=256, out_dtype=F32, name="a_in_misc")
    g0 = B_Q_RANK + B_KV_RANK
    gates = misc[:, g0:g0 + n_g]
    kpe = misc[:, g0 + n_g:g0 + n_g + B_ROPE]

    hs = _mlstm(qkv, gates, gate_bias, n_heads=a_heads, dk=A_DK, dv=A_DV, s_len=s_len, l_len=l_len,
                t=_pick(l_len, (A_CHUNK_K, 128, 64)))
    a_out = _mlstm_out(hs, o_gate, head_norm, a_heads, A_DV)

    hp = MLA_HEAD_PAD
    qd = B_NOPE + B_ROPE
    w_uq_p = jnp.pad(w_uq[j].reshape(B_Q_RANK, b_heads, qd), ((0, 0), (0, 0), (0, hp - qd))).reshape(B_Q_RANK, b_heads * hp)
    w_ukv3 = w_ukv[j].reshape(B_KV_RANK, b_heads, B_NOPE + B_VDIM)
    w_uk_p = jnp.pad(w_ukv3[:, :, :B_NOPE], ((0, 0), (0, 0), (0, hp - B_NOPE))).reshape(B_KV_RANK, b_heads * hp)
    w_uv = w_ukv3[:, :, B_NOPE:].reshape(B_KV_RANK, b_heads * B_VDIM)
    kpe_tab = jnp.tile(jnp.pad(kpe, ((0, 0), (B_NOPE, hp - qd))), (1, 2))
    tn_h = 2 * hp
    q = _mm(misc, w_uq_p, a_k=B_Q_RANK, a_col0=0, n_cols=b_heads * hp, tm=tm, tn=tn_h, out_dtype=BF16,
            prologue="rmsnorm", norm_w=q_norm, rope=mla_tabs, rope_d=mla_d, name="mla_q")
    kk = _mm(misc, w_uk_p, a_k=B_KV_RANK, a_col0=B_Q_RANK, n_cols=b_heads * hp, tm=tm, tn=tn_h, out_dtype=BF16,
             prologue="rmsnorm", norm_w=kv_norm, tab=kpe_tab, rope=mla_tabs, rope_d=mla_d, name="mla_k")
    vv = _mm(misc, w_uv, a_k=B_KV_RANK, a_col0=B_Q_RANK, n_cols=b_heads * B_VDIM, tm=tm,
             tn=_pick(b_heads * B_VDIM, (512, 256, 128)), out_dtype=BF16, prologue="rmsnorm", norm_w=kv_norm,
             name="mla_v")
    scale = qd ** -0.5
    tq = _pick(s_len, (1024, 512, 256, 128))
    tk = _pick(n, (768, 512, 384, 256, 128))
    b_x = _flash(q, kk, vv, heads=b_heads, dk=hp, dv=B_VDIM, q_col0=0, k_col0=0, v_col0=0, v_share=1, q_row0=0,
                 q_rows=s_len, tq=tq, k_row0=0, k_rows=n, tk=tk, scale=scale, out_dtype=BF16, name="mla_attn_x")
    b_c = _flash(q, kk, vv, heads=b_heads, dk=hp, dv=B_VDIM, q_col0=0, k_col0=0, v_col0=0, v_share=1, q_row0=s_len,
                 q_rows=l_len, tq=l_len, k_row0=s_len, k_rows=l_len, tk=l_len, scale=scale, out_dtype=BF16,
                 name="mla_attn_c")
    mix = jnp.concatenate([a_out, jnp.concatenate([b_x, b_c], axis=0)], axis=1)
    return _mm(mix, w_out, b_lead=j, n_cols=d, tm=tm, tn=_pick(d, (512, 256, 128)), out_dtype=F32, resid=xa,
               gate=gate, n_split=s_len, name="a_out")


def _odd_mixer(h, xa, gate, lw, s_len, l_len, tm, lam_init):
    (w_in, j, rpb, lam_p, subln, w_out, diff_tabs, diff_d) = lw
    n, d = h.shape
    half = d // 2
    c_heads = half // C_DIM
    d_heads = half // D_VDIM
    n_c = c_heads * C_DIM
    n_dq = 2 * d_heads * D_DIM
    n_dv = d_heads * D_VDIM
    tn = _pick(math.gcd(math.gcd(3 * n_c, 2 * n_dq), n_dv), (512, 256, 128))
    nat = _mm(h, w_in, b_lead=j, n_cols=3 * n_c, tm=tm, tn=tn, out_dtype=BF16, name="b_in_nat")
    dqk = _mm(h, w_in, b_lead=j, b_col0=3 * n_c, n_cols=2 * n_dq, tm=tm, tn=tn, out_dtype=BF16,
              rope=tuple(jnp.tile(tb, (1, tn // D_DIM)) for tb in diff_tabs), rope_d=diff_d, name="b_in_dqk")
    dvv = _mm(h, w_in, b_lead=j, b_col0=3 * n_c + 2 * n_dq, n_cols=n_dv, tm=tm, tn=tn, out_dtype=BF16,
              name="b_in_dv")

    rows = s_len // GRID_W
    kr = min(C_WIN_R, rows)
    tab = _natten_table(rpb[j], GRID_W, kr)
    na_x = _natten(nat, tab, heads=c_heads, d=C_DIM, q_col0=0, k_col0=n_c, v_col0=2 * n_c, s_len=s_len, l_len=l_len,
                   wg=GRID_W, kr=kr)
    na_c = _flash(nat, nat, nat, heads=c_heads, dk=C_DIM, dv=C_DIM, q_col0=0, k_col0=n_c, v_col0=2 * n_c, v_share=1,
                  q_row0=s_len, q_rows=l_len, tq=l_len, k_row0=s_len, k_rows=l_len, tk=l_len, scale=C_DIM ** -0.5,
                  out_dtype=BF16, name="nat_attn_c")

    tq = _pick(s_len, (1024, 512, 256, 128))
    tk = _pick(n, (768, 512, 384, 256, 128))
    maps = 2 * d_heads
    df_x = _flash(dqk, dqk, dvv, heads=maps, dk=D_DIM, dv=D_VDIM, q_col0=0, k_col0=n_dq, v_col0=0, v_share=2,
                  q_row0=0, q_rows=s_len, tq=tq, k_row0=0, k_rows=n, tk=tk, scale=D_DIM ** -0.5, out_dtype=F32,
                  name="diff_attn_x")
    df_c = _flash(dqk, dqk, dvv, heads=maps, dk=D_DIM, dv=D_VDIM, q_col0=0, k_col0=n_dq, v_col0=0, v_share=2,
                  q_row0=s_len, q_rows=l_len, tq=l_len, k_row0=s_len, k_rows=l_len, tk=l_len, scale=D_DIM ** -0.5,
                  out_dtype=F32, name="diff_attn_c")
    df = _diff_post(jnp.concatenate([df_x, df_c], axis=0), lam_p[j], subln[j], lam_init, d_heads, D_VDIM)
    mix = jnp.concatenate([jnp.concatenate([na_x, na_c], axis=0), df], axis=1)
    return _mm(mix, w_out, b_lead=j, n_cols=d, tm=tm, tn=_pick(d, (512, 256, 128)), out_dtype=F32, resid=xa,
               gate=gate, n_split=s_len, name="b_out")


def kernel(x, c, ctx, c_ctx, ada_down, ada_up, ada_bias, norm_mix, norm_ffn, norm_final, a_w_in, a_gate_bias,
           a_head_norm, a_q_norm, a_kv_norm, a_w_uq, a_w_ukv, a_w_out, b_w_in, b_rpb, b_lambda, b_subln, b_w_out,
           router_w, router_bias, moe_w1, moe_w3, moe_w2):
    bsz, s_len, d = x.shape
    l_len = ctx.shape[1]
    depth = ada_down.shape[0]
    assert bsz == 1
    n = s_len + l_len
    xa = jnp.concatenate([x[0], ctx[0]], axis=0)
    tm = _pick(n, (768, 512, 384, 256, 128))

    mla_tabs, mla_d = _axial_tables(s_len, l_len, MLA_HEAD_PAD, B_NOPE, B_ROPE, 2)
    diff_tabs, diff_d = _axial_tables(s_len, l_len, D_DIM, 0, D_DIM, 1)

    cvec = jnp.zeros((16, d), F32).at[0].set(c[0]).at[1].set(c_ctx)
    rank = ada_down.shape[-1]
    n_mod_cols = ada_up.shape[-1]
    for l in range(depth):
        j = l // 2
        t_low = _mm(cvec, ada_down, b_lead=l, n_cols=rank, tm=16, tn=rank, out_dtype=F32, prologue="silu",
                    name="ada_down")
        mods = _mm(t_low, ada_up, b_lead=l, n_cols=n_mod_cols, tm=16, tn=_pick(n_mod_cols, (2048, 1024, 512, 256, 128)),
                   out_dtype=F32, bias=ada_bias[l], name="ada_up")
        mods = mods[:2].reshape(2, N_MOD, d)
        m = [mods[:, i, :] for i in range(N_MOD)]
        h = _norm_mod(xa, norm_mix[l], m[0], m[1], s_len, BF16)
        if l % 2 == 0:
            lw = (a_w_in, j, a_gate_bias[j], a_head_norm[j], a_q_norm[j], a_kv_norm[j], a_w_uq, a_w_ukv, a_w_out,
                  mla_tabs, mla_d)
            xa = _even_mixer(h, xa, m[2], lw, s_len, l_len, tm)
        else:
            lam_init = 0.8 - 0.6 * math.exp(-0.3 * l)
            lw = (b_w_in, j, b_rpb, b_lambda, b_subln, b_w_out, diff_tabs, diff_d)
            xa = _odd_mixer(h, xa, m[2], lw, s_len, l_len, tm, lam_init)
        h2 = _norm_mod(xa, norm_ffn[l], m[3], m[4], s_len, F32)
        f = _moe(h2, router_w, router_bias, moe_w1, moe_w3, moe_w2, l)
        xa = _gated_add(xa, f, m[5], s_len)
    out = _rmsnorm_rows(xa, norm_final, s_len)
    return out.reshape(1, s_len, d)
```

```python
import functools
import math

import jax
import jax.numpy as jnp
from jax import lax
from jax.experimental import pallas as pl
from jax.experimental.pallas import tpu as pltpu

F32 = jnp.float32
BF16 = jnp.bfloat16

GRID_W = 64
EPS = 1e-6
NEG_INF = -1e30
ROPE_BASE = 10000.0
N_MOD = 6
A_DK = 256
A_DV = 512
B_VDIM = 128
B_NOPE = 128
B_ROPE = 64
B_Q_RANK = 1024
B_KV_RANK = 512
C_DIM = 128
C_WIN_R = 8
C_WIN_C = 16
D_DIM = 128
D_VDIM = 256
N_GROUPS = 4
TOP_K = 2

LANES = 128
VMEM_LIMIT_BYTES = 56 * 1024 * 1024
MLA_HEAD_PAD = 256

A_CHUNK_K = 256
MOE_ROWS = 512
NAT_ROWS = 8
NAT_UNROLL = 4
FLASH_ROWS = 256


def _pick(n, cands):
    for c in cands:
        if n % c == 0:
            return c
    return n


def _params(sem):
    return pltpu.CompilerParams(dimension_semantics=sem, vmem_limit_bytes=VMEM_LIMIT_BYTES)


def _mm_body(*refs, prologue, has_bias, has_tab, rope_d, has_resid, n_split, tm):
    it = iter(refs)
    a_ref = next(it)
    b_ref = next(it)
    nw_ref = next(it) if prologue == "rmsnorm" else None
    bias_ref = next(it) if has_bias else None
    tab_ref = next(it) if has_tab else None
    if rope_d:
        cos_ref, sa_ref, sb_ref = next(it), next(it), next(it)
    if has_resid:
        res_ref, gate_ref = next(it), next(it)
    o_ref = next(it)

    a = a_ref[...]
    if prologue == "silu":
        af = a.astype(F32)
        a = af * jax.nn.sigmoid(af)
    elif prologue == "rmsnorm":
        af = a.astype(F32)
        a = af * lax.rsqrt(jnp.mean(af * af, axis=-1, keepdims=True) + EPS) * nw_ref[...]
    y = jnp.dot(a.astype(BF16), b_ref[...].astype(BF16), preferred_element_type=F32)
    if has_bias:
        y = y + bias_ref[...]
    if has_tab:
        y = y + tab_ref[...]
    if rope_d:
        w = y.shape[1]
        y = y * cos_ref[...] + pltpu.roll(y, w - rope_d, 1) * sa_ref[...] + pltpu.roll(y, rope_d, 1) * sb_ref[...]
    if has_resid:
        rows = pl.program_id(0) * tm + lax.broadcasted_iota(jnp.int32, (tm, 1), 0)
        g = jnp.where(rows < n_split, gate_ref[0:1, :], gate_ref[1:2, :])
        y = res_ref[...] + g * y
    o_ref[...] = y.astype(o_ref.dtype)


def _mm(a, b, *, n_cols, tm, tn, out_dtype, b_lead=None, b_col0=0, a_k=None, a_col0=0, prologue=None,
        norm_w=None, bias=None, tab=None, rope=None, rope_d=0, resid=None, gate=None, n_split=0, name="mm"):
    m = a.shape[0]
    k = a_k if a_k is not None else a.shape[1]
    assert m % tm == 0 and n_cols % tn == 0 and b_col0 % tn == 0 and a_col0 % k == 0
    acb, bcb = a_col0 // k, b_col0 // tn
    in_specs = [pl.BlockSpec((tm, k), lambda i, j: (i, acb))]
    if b_lead is None:
        in_specs.append(pl.BlockSpec((k, tn), lambda i, j: (0, j + bcb)))
    else:
        in_specs.append(pl.BlockSpec((None, k, tn), lambda i, j: (b_lead, 0, j + bcb)))
    args = [a, b]
    if prologue == "rmsnorm":
        in_specs.append(pl.BlockSpec((1, k), lambda i, j: (0, 0)))
        args.append(norm_w.reshape(1, k).astype(F32))
    if bias is not None:
        in_specs.append(pl.BlockSpec((1, tn), lambda i, j: (0, j)))
        args.append(bias.reshape(1, n_cols).astype(F32))
    if tab is not None:
        in_specs.append(pl.BlockSpec((tm, tn), lambda i, j: (i, 0)))
        args.append(tab)
    if rope is not None:
        for t in rope:
            in_specs.append(pl.BlockSpec((tm, tn), lambda i, j: (i, 0)))
            args.append(t)
    if resid is not None:
        in_specs.append(pl.BlockSpec((tm, tn), lambda i, j: (i, j)))
        in_specs.append(pl.BlockSpec((2, tn), lambda i, j: (0, j)))
        args += [resid, gate]
    body = functools.partial(_mm_body, prologue=prologue, has_bias=bias is not None, has_tab=tab is not None,
                             rope_d=rope_d if rope is not None else 0, has_resid=resid is not None,
                             n_split=n_split, tm=tm)
    return pl.pallas_call(
        body,
        out_shape=jax.ShapeDtypeStruct((m, n_cols), out_dtype),
        grid=(m // tm, n_cols // tn),
        in_specs=in_specs,
        out_specs=pl.BlockSpec((tm, tn), lambda i, j: (i, j)),
        compiler_params=_params(("parallel", "arbitrary")),
        name=name,
    )(*args)


def _norm_mod_body(x_ref, w_ref, sh_ref, sc_ref, o_ref, *, n_split, tm):
    x = x_ref[...]
    y = x * lax.rsqrt(jnp.mean(x * x, axis=-1, keepdims=True) + EPS) * w_ref[...]
    rows = pl.program_id(0) * tm + lax.broadcasted_iota(jnp.int32, (tm, 1), 0)
    lat = rows < n_split
    sc = jnp.where(lat, sc_ref[0:1, :], sc_ref[1:2, :])
    sh = jnp.where(lat, sh_ref[0:1, :], sh_ref[1:2, :])
    o_ref[...] = (y * (1.0 + sc) + sh).astype(o_ref.dtype)


def _norm_mod(x, w, shift, scale, n_split, out_dtype):
    n, d = x.shape
    tm = _pick(n, (256, 128, 64, 32, 16, 8))
    return pl.pallas_call(
        functools.partial(_norm_mod_body, n_split=n_split, tm=tm),
        out_shape=jax.ShapeDtypeStruct((n, d), out_dtype),
        grid=(n // tm,),
        in_specs=[pl.BlockSpec((tm, d), lambda i: (i, 0)),
                  pl.BlockSpec((1, d), lambda i: (0, 0)),
                  pl.BlockSpec((2, d), lambda i: (0, 0)),
                  pl.BlockSpec((2, d), lambda i: (0, 0))],
        out_specs=pl.BlockSpec((tm, d), lambda i: (i, 0)),
        compiler_params=_params(("parallel",)),
        name="norm_mod",
    )(x, w.reshape(1, d), shift, scale)


def _rmsnorm_body(x_ref, w_ref, o_ref):
    x = x_ref[...]
    o_ref[...] = x * lax.rsqrt(jnp.mean(x * x, axis=-1, keepdims=True) + EPS) * w_ref[...]


def _rmsnorm_rows(x, w, n_rows):
    d = x.shape[1]
    tm = _pick(n_rows, (256, 128, 64, 32, 16, 8))
    return pl.pallas_call(
        _rmsnorm_body,
        out_shape=jax.ShapeDtypeStruct((n_rows, d), F32),
        grid=(n_rows // tm,),
        in_specs=[pl.BlockSpec((tm, d), lambda i: (i, 0)), pl.BlockSpec((1, d), lambda i: (0, 0))],
        out_specs=pl.BlockSpec((tm, d), lambda i: (i, 0)),
        compiler_params=_params(("parallel",)),
        name="final_norm",
    )(x, w.reshape(1, d))


def _gated_add_body(x_ref, f_ref, g_ref, o_ref, *, n_split, tm):
    rows = pl.program_id(0) * tm + lax.broadcasted_iota(jnp.int32, (tm, 1), 0)
    g = jnp.where(rows < n_split, g_ref[0:1, :], g_ref[1:2, :])
    o_ref[...] = x_ref[...] + g * f_ref[...]


def _gated_add(x, f, gate, n_split):
    n, d = x.shape
    tm = _pick(n, (256, 128, 64, 32, 16, 8))
    return pl.pallas_call(
        functools.partial(_gated_add_body, n_split=n_split, tm=tm),
        out_shape=jax.ShapeDtypeStruct((n, d), F32),
        grid=(n // tm,),
        in_specs=[pl.BlockSpec((tm, d), lambda i: (i, 0)),
                  pl.BlockSpec((tm, d), lambda i: (i, 0)),
                  pl.BlockSpec((2, d), lambda i: (0, 0))],
        out_specs=pl.BlockSpec((tm, d), lambda i: (i, 0)),
        compiler_params=_params(("parallel",)),
        name="gated_add",
    )(x, f, gate)


def _flash_body(q_ref, k_ref, v_ref, o_ref, m_sc, l_sc, acc_sc, *, c_exp, nk, rsub, dv_out, l_from_v):
    j = pl.program_id(2)
    tq, tk, dv = q_ref.shape[0], k_ref.shape[0], v_ref.shape[1]

    @pl.when(j == 0)
    def _():
        m_sc[...] = jnp.full(m_sc.shape, -jnp.inf, F32)
        l_sc[...] = jnp.zeros(l_sc.shape, F32)
        acc_sc[...] = jnp.zeros(acc_sc.shape, F32)

    k = k_ref[...]
    v = v_ref[...]
    for r in range(tq // rsub):
        sl = slice(r * rsub, (r + 1) * rsub)
        s = lax.dot_general(q_ref[sl, :], k, (((1,), (1,)), ((), ())), preferred_element_type=F32)
        m_prev = m_sc[sl, :]
        m_new = jnp.maximum(m_prev, jnp.max(s, axis=-1, keepdims=True))
        alpha = jnp.exp2((m_prev - m_new) * c_exp)
        p = jnp.exp2((s - pltpu.repeat(m_new, tk // LANES, 1)) * c_exp)
        if not l_from_v:
            l_sc[sl, :] = alpha * l_sc[sl, :] + jnp.sum(p, axis=-1, keepdims=True)
        acc_sc[sl, :] = (pltpu.repeat(alpha, dv // LANES, 1) * acc_sc[sl, :]
                         + jnp.dot(p.astype(BF16), v, preferred_element_type=F32))
        m_sc[sl, :] = m_new

    @pl.when(j == nk - 1)
    def _():
        acc = acc_sc[...]
        if l_from_v:
            o = acc[:, :dv_out] / acc[:, dv_out:2 * dv_out]
        else:
            o = acc / pltpu.repeat(l_sc[...], dv // LANES, 1)
        o_ref[...] = o.astype(o_ref.dtype)


def _flash(q, k, v, *, heads, dk, dv, q_col0, k_col0, v_col0, v_share, q_row0, q_rows, tq, k_row0, k_rows, tk,
           scale, out_dtype, name, l_from_v=False):
    assert q_rows % tq == 0 and k_rows % tk == 0 and q_row0 % tq == 0 and k_row0 % tk == 0
    assert q_col0 % dk == 0 and k_col0 % dk == 0 and v_col0 % dv == 0 and tk % LANES == 0 and dv % LANES == 0
    qrb, krb, qcb, kcb, vcb = q_row0 // tq, k_row0 // tk, q_col0 // dk, k_col0 // dk, v_col0 // dv
    nq, nk = q_rows // tq, k_rows // tk
    dv_out = dv // 2 if l_from_v else dv
    rsub = _pick(tq, (FLASH_ROWS, 128, 64, 32, 16, 8))
    return pl.pallas_call(
        functools.partial(_flash_body, c_exp=scale * math.log2(math.e), nk=nk, rsub=rsub, dv_out=dv_out,
                          l_from_v=l_from_v),
        out_shape=jax.ShapeDtypeStruct((q_rows, heads * dv_out), out_dtype),
        grid=(heads, nq, nk),
        in_specs=[pl.BlockSpec((tq, dk), lambda h, i, j: (i + qrb, h + qcb)),
                  pl.BlockSpec((tk, dk), lambda h, i, j: (j + krb, h + kcb)),
                  pl.BlockSpec((tk, dv), lambda h, i, j: (j + krb, h // v_share + vcb))],
        out_specs=pl.BlockSpec((tq, dv_out), lambda h, i, j: (i, h)),
        scratch_shapes=[pltpu.VMEM((tq, LANES), F32), pltpu.VMEM((tq, LANES), F32), pltpu.VMEM((tq, dv), F32)],
        compiler_params=_params(("parallel", "parallel", "arbitrary")),
        name=name,
    )(q, k, v)


def _diff_post_body(o_ref, lam_ref, w_ref, out_ref, *, lam_init, dv):
    lp = lam_ref[...]
    lam = (jnp.exp(jnp.sum(lp[0:1, :] * lp[1:2, :], axis=-1, keepdims=True))
           - jnp.exp(jnp.sum(lp[2:3, :] * lp[3:4, :], axis=-1, keepdims=True)) + lam_init)
    o = o_ref[...]
    d = o[:, :dv] - lam * o[:, dv:]
    y = d * lax.rsqrt(jnp.mean(d * d, axis=-1, keepdims=True) + EPS) * w_ref[...]
    out_ref[...] = (y * (1.0 - lam_init)).astype(out_ref.dtype)


def _diff_post(o, lam_p, subln, lam_init, heads, dv):
    n = o.shape[0]
    tm = _pick(n, (256, 128, 64, 32, 16, 8))
    return pl.pallas_call(
        functools.partial(_diff_post_body, lam_init=lam_init, dv=dv),
        out_shape=jax.ShapeDtypeStruct((n, heads * dv), BF16),
        grid=(n // tm, heads),
        in_specs=[pl.BlockSpec((tm, 2 * dv), lambda i, h: (i, h)),
                  pl.BlockSpec(lam_p.shape, lambda i, h: (0, 0)),
                  pl.BlockSpec((1, dv), lambda i, h: (0, 0))],
        out_specs=pl.BlockSpec((tm, dv), lambda i, h: (i, h)),
        compiler_params=_params(("parallel", "parallel")),
        name="diff_post",
    )(o, lam_p.astype(F32), subln.reshape(1, dv).astype(F32))


def _natten_body(q_ref, k_ref, v_ref, kc_ref, vc_ref, tab_ref, o_ref, *, scale, rows, kr, wg, rb):
    blk = pl.program_id(1)
    kc = kc_ref[...]
    vc = vc_ref[...]

    def one_row(rr, carry):
        i = blk * rb + rr
        r0 = jnp.clip(i - kr // 2, 0, rows - kr)
        idx0 = r0 - i + (kr - 1)
        q = q_ref[pl.ds(pl.multiple_of(rr * wg, wg), wg), :]
        koff = pl.multiple_of(r0 * wg, wg)
        kw = k_ref[pl.ds(koff, kr * wg), :]
        vw = v_ref[pl.ds(koff, kr * wg), :]
        s_loc = lax.dot_general(q, kw, (((1,), (1,)), ((), ())), preferred_element_type=F32) * scale + tab_ref[idx0]
        s_ctx = lax.dot_general(q, kc, (((1,), (1,)), ((), ())), preferred_element_type=F32) * scale
        m = jnp.maximum(jnp.max(s_loc, axis=-1, keepdims=True), jnp.max(s_ctx, axis=-1, keepdims=True))
        p_loc = jnp.exp(s_loc - m)
        p_ctx = jnp.exp(s_ctx - m)
        l = jnp.sum(p_loc, axis=-1, keepdims=True) + jnp.sum(p_ctx, axis=-1, keepdims=True)
        o = (jnp.dot(p_ctx.astype(BF16), vc, preferred_element_type=F32)
             + jnp.dot(p_loc.astype(BF16), vw, preferred_element_type=F32)) / l
        o_ref[pl.ds(pl.multiple_of(rr * wg, wg), wg), :] = o.astype(o_ref.dtype)
        return carry

    lax.fori_loop(0, rb, one_row, 0, unroll=math.gcd(rb, NAT_UNROLL))


def _natten(p, tab, *, heads, d, q_col0, k_col0, v_col0, s_len, l_len, wg, kr):
    rows = s_len // wg
    rb = _pick(rows, (NAT_ROWS, 4, 2, 1))
    qcb, kcb, vcb = q_col0 // d, k_col0 // d, v_col0 // d
    crb = s_len // l_len
    assert s_len % l_len == 0
    return pl.pallas_call(
        functools.partial(_natten_body, scale=d ** -0.5, rows=rows, kr=kr, wg=wg, rb=rb),
        out_shape=jax.ShapeDtypeStruct((s_len, heads * d), BF16),
        grid=(heads, rows // rb),
        in_specs=[pl.BlockSpec((rb * wg, d), lambda h, i: (i, h + qcb)),
                  pl.BlockSpec((s_len, d), lambda h, i: (0, h + kcb)),
                  pl.BlockSpec((s_len, d), lambda h, i: (0, h + vcb)),
                  pl.BlockSpec((l_len, d), lambda h, i: (crb, h + kcb)),
                  pl.BlockSpec((l_len, d), lambda h, i: (crb, h + vcb)),
                  pl.BlockSpec((None, kr, wg, kr * wg), lambda h, i: (h, 0, 0, 0))],
        out_specs=pl.BlockSpec((rb * wg, d), lambda h, i: (i, h)),
        compiler_params=_params(("parallel", "arbitrary")),
        name="natten",
    )(p, p, p, p, p, tab)


def _natten_table(rpb, wg, kr):
    cols = jnp.arange(wg)
    c0 = jnp.clip(cols - C_WIN_C // 2, 0, wg - C_WIN_C)
    col_ok = (cols[None, :] >= c0[:, None]) & (cols[None, :] < c0[:, None] + C_WIN_C)
    dc_idx = jnp.clip(cols[None, :] - cols[:, None] + C_WIN_C - 1, 0, 2 * C_WIN_C - 2)
    bias_c = rpb.astype(F32)[:, :, dc_idx]
    dr = (jnp.arange(kr)[:, None] - (kr - 1)) + jnp.arange(kr)[None, :] + C_WIN_R - 1
    t = bias_c[:, dr]
    t = jnp.where(col_ok[None, None, None], t, NEG_INF)
    t = t.transpose(0, 1, 3, 2, 4)
    return t.reshape(t.shape[0], kr, wg, kr * wg)


def _log_sigmoid(x):
    return jnp.minimum(x, 0.0) - jnp.log1p(jnp.exp(-jnp.abs(x)))


def _mlstm_body(q_ref, k_ref, v_ref, gr_ref, gc_ref, br_ref, bc_ref, o_ref, ct_sc, n_sc, m_sc, *, t, n_heads, qscale):
    dh = pl.program_id(0)
    c = pl.program_id(1)
    backward = dh >= n_heads

    @pl.when(c == 0)
    def _():
        ct_sc[...] = jnp.zeros(ct_sc.shape, F32)
        n_sc[...] = jnp.zeros(n_sc.shape, F32)
        m_sc[...] = jnp.zeros(m_sc.shape, F32)

    gr = gr_ref[...] + br_ref[...]
    gc = gc_ref[...] + bc_ref[...]
    ig_r, lf_r = gr[0:1, :], _log_sigmoid(gr[1:2, :])
    ig_c, lf_c = gc[:, 0:1], _log_sigmoid(gc[:, 1:2])

    jj = lax.broadcasted_iota(jnp.int32, (t, t), 0)
    ss = lax.broadcasted_iota(jnp.int32, (t, t), 1)
    sgn = jnp.where(backward, -1, 1)
    incl = (ss - jj) * sgn <= 0
    incl_t = (jj - ss) * sgn <= 0
    b_c = jnp.sum(jnp.where(incl, lf_r, 0.0), axis=1, keepdims=True)
    b_r = jnp.sum(jnp.where(incl_t, lf_c, 0.0), axis=0, keepdims=True)
    total = jnp.sum(lf_r, axis=1, keepdims=True)

    m_prev = m_sc[...]
    w_end_r = total - b_r + ig_r
    w_end_c = total - b_c + ig_c
    m_new = jnp.maximum(total + m_prev, jnp.max(w_end_r, axis=1, keepdims=True))
    decay = jnp.exp(total + m_prev - m_new)
    ws_c = jnp.exp(w_end_c - m_new)

    q = q_ref[...] * qscale
    k = k_ref[...]
    v = v_ref[...]
    ct = ct_sc[...]
    n_row = n_sc[...]

    logw = jnp.where(incl, b_c - b_r + ig_r, NEG_INF)
    inter = b_c + m_prev
    m_row = jnp.maximum(inter, jnp.max(logw, axis=1, keepdims=True))
    qk = lax.dot_general(q, k, (((1,), (1,)), ((), ())), preferred_element_type=F32)
    amat = jnp.exp(logw - m_row) * qk
    e_inter = jnp.exp(inter - m_row)
    num = (jnp.dot(amat.astype(BF16), v, preferred_element_type=F32)
           + e_inter * jnp.dot(q, ct.astype(BF16), preferred_element_type=F32))
    qf = q.astype(F32)
    den = jnp.sum(amat, axis=1, keepdims=True) + e_inter * jnp.sum(qf * n_row, axis=1, keepdims=True)
    o_ref[...] = num / jnp.maximum(jnp.abs(den), jnp.exp(-m_row))

    kf = k.astype(F32)
    vws = (v.astype(F32) * ws_c).astype(BF16)
    ct_sc[...] = decay * ct + lax.dot_general(k, vws, (((0,), (0,)), ((), ())), preferred_element_type=F32)
    n_sc[...] = decay * n_row + jnp.sum(kf * ws_c, axis=0, keepdims=True)
    m_sc[...] = m_new


def _mlstm(qkv, gates, gate_bias, *, n_heads, dk, dv, s_len, l_len, t):
    n = s_len + l_len
    nx, nc = s_len // t, l_len // t
    assert s_len % t == 0 and l_len % t == 0
    g4 = gates.reshape(n, 2, 2, n_heads)
    g_sel = g4.transpose(1, 3, 0, 2).reshape(2 * n_heads, n, 2)
    g_col = g_sel
    g_row = g_sel.reshape(2 * n_heads, n // t, t, 2).transpose(0, 1, 3, 2)
    b4 = gate_bias.astype(F32).reshape(2, 2, n_heads).transpose(0, 2, 1).reshape(2 * n_heads, 2)
    b_row = b4.reshape(2 * n_heads, 2, 1)
    b_col = b4.reshape(2 * n_heads, 1, 2)

    def blk(dh, c):
        fwd = jnp.where(c < nc, nx + c, c - nc)
        bwd = jnp.where(c < nc, nx + nc - 1 - c, nx - 1 - (c - nc))
        return jnp.where(dh >= n_heads, bwd, fwd)

    kcb, vcb = n_heads, 2 * n_heads * dk // dv
    return pl.pallas_call(
        functools.partial(_mlstm_body, t=t, n_heads=n_heads, qscale=dk ** -0.5),
        out_shape=jax.ShapeDtypeStruct((2, n, n_heads * dv), F32),
        grid=(2 * n_heads, nx + nc),
        in_specs=[pl.BlockSpec((t, dk), lambda dh, c: (blk(dh, c), dh % n_heads)),
                  pl.BlockSpec((t, dk), lambda dh, c: (blk(dh, c), kcb + dh % n_heads)),
                  pl.BlockSpec((t, dv), lambda dh, c: (blk(dh, c), vcb + dh % n_heads)),
                  pl.BlockSpec((None, None, 2, t), lambda dh, c: (dh, blk(dh, c), 0, 0)),
                  pl.BlockSpec((None, t, 2), lambda dh, c: (dh, blk(dh, c), 0)),
                  pl.BlockSpec((None, 2, 1), lambda dh, c: (dh, 0, 0)),
                  pl.BlockSpec((None, 1, 2), lambda dh, c: (dh, 0, 0))],
        out_specs=pl.BlockSpec((None, t, dv), lambda dh, c: (dh // n_heads, blk(dh, c), dh % n_heads)),
        scratch_shapes=[pltpu.VMEM((dk, dv), F32), pltpu.VMEM((1, dk), F32), pltpu.VMEM((1, 1), F32)],
        compiler_params=_params(("parallel", "arbitrary")),
        name="mlstm",
    )(qkv, qkv, qkv, g_row, g_col, b_row, b_col)


def _mlstm_out_body(hf_ref, hb_ref, o_ref, w_ref, out_ref):
    h = hf_ref[...] + hb_ref[...]
    hn = h * lax.rsqrt(jnp.mean(h * h, axis=-1, keepdims=True) + EPS) * w_ref[...]
    out_ref[...] = (hn * jax.nn.sigmoid(o_ref[...])).astype(out_ref.dtype)


def _mlstm_out(hs, o, w, n_heads, dv):
    n = o.shape[0]
    tm = _pick(n, (256, 128, 64, 32, 16, 8))
    return pl.pallas_call(
        _mlstm_out_body,
        out_shape=jax.ShapeDtypeStruct((n, n_heads * dv), BF16),
        grid=(n // tm, n_heads),
        in_specs=[pl.BlockSpec((None, tm, dv), lambda i, h: (0, i, h)),
                  pl.BlockSpec((None, tm, dv), lambda i, h: (1, i, h)),
                  pl.BlockSpec((tm, dv), lambda i, h: (i, h)),
                  pl.BlockSpec((1, dv), lambda i, h: (0, h))],
        out_specs=pl.BlockSpec((tm, dv), lambda i, h: (i, h)),
        compiler_params=_params(("parallel", "parallel")),
        name="mlstm_out",
    )(hs, hs, o, w.reshape(1, n_heads * dv).astype(F32))


def _route(h, wt_ref, b_ref, e_ref, g_ref, n_exp, n_groups):
    logits = lax.dot_general(wt_ref[...], h, (((1,), (1,)), ((), ())), preferred_element_type=F32,
                             precision=lax.Precision.HIGHEST)
    scores = jax.nn.sigmoid(logits)
    sel = scores + b_ref[...]
    epg = n_exp // n_groups
    sc = [scores[e:e + 1, :] for e in range(n_exp)]
    sl = [sel[e:e + 1, :] for e in range(n_exp)]

    def top2_sum(vals):
        best = jnp.maximum(vals[0], vals[1])
        second = jnp.minimum(vals[0], vals[1])
        for x in vals[2:]:
            second = jnp.maximum(second, jnp.minimum(best, x))
            best = jnp.maximum(best, x)
        return best + second

    gs = [top2_sum(sl[g * epg:(g + 1) * epg]) for g in range(n_groups)]
    g_best, g_idx = gs[0], jnp.zeros_like(gs[0], dtype=jnp.int32)
    for g in range(1, n_groups):
        better = gs[g] > g_best
        g_best = jnp.where(better, gs[g], g_best)
        g_idx = jnp.where(better, g, g_idx)
    in_sel, in_raw = [], []
    for j in range(epg):
        vs, vr = sl[j], sc[j]
        for g in range(1, n_groups):
            vs = jnp.where(g_idx == g, sl[g * epg + j], vs)
            vr = jnp.where(g_idx == g, sc[g * epg + j], vr)
        in_sel.append(vs)
        in_raw.append(vr)
    v1, i1, r1 = in_sel[0], jnp.zeros_like(g_idx), in_raw[0]
    for j in range(1, epg):
        better = in_sel[j] > v1
        v1 = jnp.where(better, in_sel[j], v1)
        i1 = jnp.where(better, j, i1)
        r1 = jnp.where(better, in_raw[j], r1)
    v2 = jnp.full_like(v1, -jnp.inf)
    i2, r2 = jnp.zeros_like(g_idx), jnp.zeros_like(r1)
    for j in range(epg):
        better = (in_sel[j] > v2) & (i1 != j)
        v2 = jnp.where(better, in_sel[j], v2)
        i2 = jnp.where(better, j, i2)
        r2 = jnp.where(better, in_raw[j], r2)
    tot = r1 + r2
    e_ref[...] = jnp.concatenate([g_idx * epg + i1, g_idx * epg + i2], axis=0)
    g_ref[...] = jnp.concatenate([r1 / tot, r2 / tot], axis=0)


def _norm_route_body(x_ref, w_ref, sh_ref, sc_ref, wt_ref, b_ref, o_ref, e_ref, g_ref, *, n_split, tm, n_exp,
                     n_groups):
    x = x_ref[...]
    y = x * lax.rsqrt(jnp.mean(x * x, axis=-1, keepdims=True) + EPS) * w_ref[...]
    rows = pl.program_id(0) * tm + lax.broadcasted_iota(jnp.int32, (tm, 1), 0)
    lat = rows < n_split
    sc = jnp.where(lat, sc_ref[0:1, :], sc_ref[1:2, :])
    sh = jnp.where(lat, sh_ref[0:1, :], sh_ref[1:2, :])
    h = y * (1.0 + sc) + sh
    o_ref[...] = h.astype(o_ref.dtype)
    _route(h, wt_ref, b_ref, e_ref, g_ref, n_exp, n_groups)


def _norm_route(x, w, shift, scale, n_split, router_w, router_bias):
    n, d = x.shape
    n_exp = router_w.shape[1]
    tm = _pick(n, (256, 128))
    return pl.pallas_call(
        functools.partial(_norm_route_body, n_split=n_split, tm=tm, n_exp=n_exp, n_groups=N_GROUPS),
        out_shape=(jax.ShapeDtypeStruct((n, d), BF16),
                   jax.ShapeDtypeStruct((TOP_K, n), jnp.int32), jax.ShapeDtypeStruct((TOP_K, n), F32)),
        grid=(n // tm,),
        in_specs=[pl.BlockSpec((tm, d), lambda i: (i, 0)),
                  pl.BlockSpec((1, d), lambda i: (0, 0)),
                  pl.BlockSpec((2, d), lambda i: (0, 0)),
                  pl.BlockSpec((2, d), lambda i: (0, 0)),
                  pl.BlockSpec((n_exp, d), lambda i: (0, 0)),
                  pl.BlockSpec((n_exp, 1), lambda i: (0, 0))],
        out_specs=(pl.BlockSpec((tm, d), lambda i: (i, 0)),
                   pl.BlockSpec((TOP_K, tm), lambda i: (0, i)), pl.BlockSpec((TOP_K, tm), lambda i: (0, i))),
        compiler_params=_params(("parallel",)),
        name="norm_route",
    )(x, w.reshape(1, d), shift, scale, router_w.T.astype(F32), router_bias.reshape(n_exp, 1).astype(F32))


def _experts_body(be_ref, bv_ref, x_ref, w1_ref, w3_ref, w2_ref, rw_ref, o_ref):
    b = pl.program_id(0)
    f = pl.program_id(1)

    @pl.when(f == 0)
    def _():
        o_ref[...] = jnp.zeros(o_ref.shape, F32)

    @pl.when(bv_ref[b] > 0)
    def _():
        x = x_ref[...]
        h1 = jnp.dot(x, w1_ref[...].astype(BF16), preferred_element_type=F32)
        h3 = jnp.dot(x, w3_ref[...].astype(BF16), preferred_element_type=F32)
        act = (h1 * jax.nn.sigmoid(h1) * h3).astype(BF16)
        y = jnp.dot(act, w2_ref[...].astype(BF16), preferred_element_type=F32)
        o_ref[...] += y * rw_ref[...]


def _experts(xb, row_w, blk_e, blk_valid, w1, w3, w2, layer, tb, fc):
    n_rows, d = xb.shape
    d_exp = w1.shape[-1]
    nf = d_exp // fc
    nb = n_rows // tb

    def fsel(b, f, bv):
        return jnp.where(bv[b] > 0, f, nf - 1)

    grid_spec = pltpu.PrefetchScalarGridSpec(
        num_scalar_prefetch=2,
        grid=(nb, nf),
        in_specs=[pl.BlockSpec((tb, d), lambda b, f, be, bv: (b, 0)),
                  pl.BlockSpec((None, None, d, fc), lambda b, f, be, bv: (layer, be[b], 0, fsel(b, f, bv))),
                  pl.BlockSpec((None, None, d, fc), lambda b, f, be, bv: (layer, be[b], 0, fsel(b, f, bv))),
                  pl.BlockSpec((None, None, fc, d), lambda b, f, be, bv: (layer, be[b], fsel(b, f, bv), 0)),
                  pl.BlockSpec((tb, 1), lambda b, f, be, bv: (b, 0))],
        out_specs=pl.BlockSpec((tb, d), lambda b, f, be, bv: (b, 0)),
    )
    return pl.pallas_call(
        _experts_body,
        out_shape=jax.ShapeDtypeStruct((n_rows, d), F32),
        grid_spec=grid_spec,
        compiler_params=_params(("arbitrary", "arbitrary")),
        name="experts",
    )(blk_e, blk_valid, xb, w1, w3, w2, row_w)


def _moe(h, expert, gate, n_exp, w1, w3, w2, layer):
    t_tok, d = h.shape
    tb = MOE_ROWS
    n_assign = t_tok * TOP_K
    e_flat = expert.T.reshape(-1)
    w_flat = gate.T.reshape(-1)
    onehot = (e_flat[:, None] == jnp.arange(n_exp, dtype=jnp.int32)[None, :]).astype(jnp.int32)
    rank = jnp.take_along_axis(jnp.cumsum(onehot, axis=0) - onehot, e_flat[:, None], axis=1)[:, 0]
    counts = jnp.sum(onehot, axis=0)
    padded = (counts + tb - 1) // tb * tb
    pends = jnp.cumsum(padded)
    dest = (pends - padded)[e_flat] + rank
    nb = (n_assign + tb - 1) // tb + n_exp
    n_rows = nb * tb
    tok = jnp.arange(n_assign, dtype=jnp.int32) // TOP_K
    buf_tok = jnp.zeros((n_rows,), jnp.int32).at[dest].set(tok)
    buf_w = jnp.zeros((n_rows,), F32).at[dest].set(w_flat)
    blk_start = jnp.arange(nb, dtype=jnp.int32) * tb
    blk_e = jnp.minimum(jnp.searchsorted(pends, blk_start, side="right"), n_exp - 1).astype(jnp.int32)
    blk_valid = (blk_start < pends[-1]).astype(jnp.int32)
    xb = jnp.take(h, buf_tok, axis=0)
    fc = _pick(w1.shape[-1], (256, 128))
    yb = _experts(xb, buf_w.reshape(n_rows, 1), blk_e, blk_valid, w1, w3, w2, layer, tb, fc)
    d2 = dest.reshape(t_tok, TOP_K)
    return jnp.take(yb, d2[:, 0], axis=0) + jnp.take(yb, d2[:, 1], axis=0)


def _axial_tables(s_len, l_len, head_w, pe_off, pe_dim, reps):
    t = jnp.arange(s_len, dtype=jnp.int32)
    pos = jnp.stack([t // GRID_W, t % GRID_W], axis=0).astype(F32)
    h = pe_dim // 2
    d = h // 2
    inv = ROPE_BASE ** (-jnp.arange(d, dtype=F32) / d)
    lane = jnp.arange(pe_dim)
    sub, rr = lane // h, lane % h
    ang = pos[sub, :].T * inv[rr % d][None, :]
    first = (rr < d)[None, :]
    cos = jnp.ones((s_len, head_w), F32).at[:, pe_off:pe_off + pe_dim].set(jnp.cos(ang))
    sa = jnp.zeros((s_len, head_w), F32).at[:, pe_off:pe_off + pe_dim].set(jnp.where(first, -jnp.sin(ang), 0.0))
    sb = jnp.zeros((s_len, head_w), F32).at[:, pe_off:pe_off + pe_dim].set(jnp.where(first, 0.0, jnp.sin(ang)))
    cos = jnp.concatenate([cos, jnp.ones((l_len, head_w), F32)], axis=0)
    sa = jnp.concatenate([sa, jnp.zeros((l_len, head_w), F32)], axis=0)
    sb = jnp.concatenate([sb, jnp.zeros((l_len, head_w), F32)], axis=0)
    return tuple(jnp.tile(a, (1, reps)) for a in (cos, sa, sb)), d


def _even_mixer(h, xa, gate, lw, s_len, l_len, tm):
    (w_in, j, gate_bias, head_norm, q_norm, kv_norm, w_uq, w_ukv, w_out, mla_tabs, mla_d) = lw
    n, d = h.shape
    half = d // 2
    a_heads = half // A_DV
    b_heads = half // B_VDIM
    n_qk = 2 * a_heads * A_DK
    n_v = a_heads * A_DV
    qkv = _mm(h, w_in, b_lead=j, n_cols=n_qk + n_v, tm=tm, tn=_pick(n_qk + n_v, (512, 256, 128)), out_dtype=BF16,
              name="a_in_qkv")
    o_gate = _mm(h, w_in, b_lead=j, b_col0=n_qk + n_v, n_cols=n_v, tm=tm, tn=_pick(n_v, (512, 256, 128)),
                 out_dtype=F32, name="a_in_o")
    off = n_qk + 2 * n_v
    n_g = 4 * a_heads
    w_misc = jnp.concatenate([
        w_in[j][:, off + n_g:off + n_g + B_Q_RANK + B_KV_RANK],
        w_in[j][:, off:off + n_g],
        w_in[j][:, off + n_g + B_Q_RANK + B_KV_RANK:]], axis=1)
    n_misc = w_misc.shape[1]
    n_misc_pad = -(-n_misc // 256) * 256
    w_misc = jnp.pad(w_misc, ((0, 0), (0, n_misc_pad - n_misc)))
    misc = _mm(h, w_misc, n_cols=n_misc_pad, tm=tm, tn=256, out_dtype=F32, name="a_in_misc")
    g0 = B_Q_RANK + B_KV_RANK
    gates = misc[:, g0:g0 + n_g]
    kpe = misc[:, g0 + n_g:g0 + n_g + B_ROPE]

    hs = _mlstm(qkv, gates, gate_bias, n_heads=a_heads, dk=A_DK, dv=A_DV, s_len=s_len, l_len=l_len,
                t=_pick(l_len, (A_CHUNK_K, 128, 64)))
    a_out = _mlstm_out(hs, o_gate, head_norm, a_heads, A_DV)

    hp = MLA_HEAD_PAD
    qd = B_NOPE + B_ROPE
    w_uq_p = jnp.pad(w_uq[j].reshape(B_Q_RANK, b_heads, qd), ((0, 0), (0, 0), (0, hp - qd))).reshape(B_Q_RANK, b_heads * hp)
    w_ukv3 = w_ukv[j].reshape(B_KV_RANK, b_heads, B_NOPE + B_VDIM)
    w_uk_p = jnp.pad(w_ukv3[:, :, :B_NOPE], ((0, 0), (0, 0), (0, hp - B_NOPE))).reshape(B_KV_RANK, b_heads * hp)
    w_uv = jnp.pad(w_ukv3[:, :, B_NOPE:], ((0, 0), (0, 0), (0, B_VDIM))).reshape(B_KV_RANK, b_heads * 2 * B_VDIM)
    v_ones = jnp.tile(jnp.concatenate([jnp.zeros((B_VDIM,), F32), jnp.ones((B_VDIM,), F32)]), b_heads)
    kpe_tab = jnp.tile(jnp.pad(kpe, ((0, 0), (B_NOPE, hp - qd))), (1, 2))
    tn_h = 2 * hp
    q = _mm(misc, w_uq_p, a_k=B_Q_RANK, a_col0=0, n_cols=b_heads * hp, tm=tm, tn=tn_h, out_dtype=BF16,
            prologue="rmsnorm", norm_w=q_norm, rope=mla_tabs, rope_d=mla_d, name="mla_q")
    kk = _mm(misc, w_uk_p, a_k=B_KV_RANK, a_col0=B_Q_RANK, n_cols=b_heads * hp, tm=tm, tn=tn_h, out_dtype=BF16,
             prologue="rmsnorm", norm_w=kv_norm, tab=kpe_tab, rope=mla_tabs, rope_d=mla_d, name="mla_k")
    vv = _mm(misc, w_uv, a_k=B_KV_RANK, a_col0=B_Q_RANK, n_cols=b_heads * 2 * B_VDIM, tm=tm,
             tn=_pick(b_heads * 2 * B_VDIM, (512, 256)), out_dtype=BF16, prologue="rmsnorm", norm_w=kv_norm,
             bias=v_ones, name="mla_v")
    scale = qd ** -0.5
    tq = _pick(s_len, (1024, 512, 256, 128))
    tk = _pick(n, (768, 512, 384, 256, 128))
    b_x = _flash(q, kk, vv, heads=b_heads, dk=hp, dv=2 * B_VDIM, q_col0=0, k_col0=0, v_col0=0, v_share=1, q_row0=0,
                 q_rows=s_len, tq=tq, k_row0=0, k_rows=n, tk=tk, scale=scale, out_dtype=BF16, l_from_v=True,
                 name="mla_attn_x")
    b_c = _flash(q, kk, vv, heads=b_heads, dk=hp, dv=2 * B_VDIM, q_col0=0, k_col0=0, v_col0=0, v_share=1,
                 q_row0=s_len, q_rows=l_len, tq=l_len, k_row0=s_len, k_rows=l_len, tk=l_len, scale=scale,
                 out_dtype=BF16, l_from_v=True, name="mla_attn_c")
    mix = jnp.concatenate([a_out, jnp.concatenate([b_x, b_c], axis=0)], axis=1)
    return _mm(mix, w_out, b_lead=j, n_cols=d, tm=tm, tn=_pick(d, (512, 256, 128)), out_dtype=F32, resid=xa,
               gate=gate, n_split=s_len, name="a_out")


def _odd_mixer(h, xa, gate, lw, s_len, l_len, tm, lam_init):
    (w_in, j, rpb, lam_p, subln, w_out, diff_tabs, diff_d) = lw
    n, d = h.shape
    half = d // 2
    c_heads = half // C_DIM
    d_heads = half // D_VDIM
    n_c = c_heads * C_DIM
    n_dq = 2 * d_heads * D_DIM
    n_dv = d_heads * D_VDIM
    tn = _pick(math.gcd(math.gcd(3 * n_c, 2 * n_dq), n_dv), (512, 256, 128))
    nat = _mm(h, w_in, b_lead=j, n_cols=3 * n_c, tm=tm, tn=tn, out_dtype=BF16, name="b_in_nat")
    dqk = _mm(h, w_in, b_lead=j, b_col0=3 * n_c, n_cols=2 * n_dq, tm=tm, tn=tn, out_dtype=BF16,
              rope=tuple(jnp.tile(tb, (1, tn // D_DIM)) for tb in diff_tabs), rope_d=diff_d, name="b_in_dqk")
    dvv = _mm(h, w_in, b_lead=j, b_col0=3 * n_c + 2 * n_dq, n_cols=n_dv, tm=tm, tn=tn, out_dtype=BF16,
              name="b_in_dv")

    rows = s_len // GRID_W
    kr = min(C_WIN_R, rows)
    tab = _natten_table(rpb[j], GRID_W, kr)
    na_x = _natten(nat, tab, heads=c_heads, d=C_DIM, q_col0=0, k_col0=n_c, v_col0=2 * n_c, s_len=s_len, l_len=l_len,
                   wg=GRID_W, kr=kr)
    na_c = _flash(nat, nat, nat, heads=c_heads, dk=C_DIM, dv=C_DIM, q_col0=0, k_col0=n_c, v_col0=2 * n_c, v_share=1,
                  q_row0=s_len, q_rows=l_len, tq=l_len, k_row0=s_len, k_rows=l_len, tk=l_len, scale=C_DIM ** -0.5,
                  out_dtype=BF16, name="nat_attn_c")

    tq = _pick(s_len, (1024, 512, 256, 128))
    tk = _pick(n, (768, 512, 384, 256, 128))
    maps = 2 * d_heads
    df_x = _flash(dqk, dqk, dvv, heads=maps, dk=D_DIM, dv=D_VDIM, q_col0=0, k_col0=n_dq, v_col0=0, v_share=2,
                  q_row0=0, q_rows=s_len, tq=tq, k_row0=0, k_rows=n, tk=tk, scale=D_DIM ** -0.5, out_dtype=F32,
                  name="diff_attn_x")
    df_c = _flash(dqk, dqk, dvv, heads=maps, dk=D_DIM, dv=D_VDIM, q_col0=0, k_col0=n_dq, v_col0=0, v_share=2,
                  q_row0=s_len, q_rows=l_len, tq=l_len, k_row0=s_len, k_rows=l_len, tk=l_len, scale=D_DIM ** -0.5,
                  out_dtype=F32, name="diff_attn_c")
    df = _diff_post(jnp.concatenate([df_x, df_c], axis=0), lam_p[j], subln[j], lam_init, d_heads, D_VDIM)
    mix = jnp.concatenate([jnp.concatenate([na_x, na_c], axis=0), df], axis=1)
    return _mm(mix, w_out, b_lead=j, n_cols=d, tm=tm, tn=_pick(d, (512, 256, 128)), out_dtype=F32, resid=xa,
               gate=gate, n_split=s_len, name="b_out")


def kernel(x, c, ctx, c_ctx, ada_down, ada_up, ada_bias, norm_mix, norm_ffn, norm_final, a_w_in, a_gate_bias,
           a_head_norm, a_q_norm, a_kv_norm, a_w_uq, a_w_ukv, a_w_out, b_w_in, b_rpb, b_lambda, b_subln, b_w_out,
           router_w, router_bias, moe_w1, moe_w3, moe_w2):
    bsz, s_len, d = x.shape
    l_len = ctx.shape[1]
    depth = ada_down.shape[0]
    assert bsz == 1
    n = s_len + l_len
    xa = jnp.concatenate([x[0], ctx[0]], axis=0)
    tm = _pick(n, (768, 512, 384, 256, 128))

    mla_tabs, mla_d = _axial_tables(s_len, l_len, MLA_HEAD_PAD, B_NOPE, B_ROPE, 2)
    diff_tabs, diff_d = _axial_tables(s_len, l_len, D_DIM, 0, D_DIM, 1)

    cvec = jnp.zeros((16, d), F32).at[0].set(c[0]).at[1].set(c_ctx)
    rank = ada_down.shape[-1]
    n_mod_cols = ada_up.shape[-1]
    for l in range(depth):
        j = l // 2
        t_low = _mm(cvec, ada_down, b_lead=l, n_cols=rank, tm=16, tn=rank, out_dtype=F32, prologue="silu",
                    name="ada_down")
        mods = _mm(t_low, ada_up, b_lead=l, n_cols=n_mod_cols, tm=16, tn=_pick(n_mod_cols, (2048, 1024, 512, 256, 128)),
                   out_dtype=F32, bias=ada_bias[l], name="ada_up")
        mods = mods[:2].reshape(2, N_MOD, d)
        m = [mods[:, i, :] for i in range(N_MOD)]
        h = _norm_mod(xa, norm_mix[l], m[0], m[1], s_len, BF16)
        if l % 2 == 0:
            lw = (a_w_in, j, a_gate_bias[j], a_head_norm[j], a_q_norm[j], a_kv_norm[j], a_w_uq, a_w_ukv, a_w_out,
                  mla_tabs, mla_d)
            xa = _even_mixer(h, xa, m[2], lw, s_len, l_len, tm)
        else:
            lam_init = 0.8 - 0.6 * math.exp(-0.3 * l)
            lw = (b_w_in, j, b_rpb, b_lambda, b_subln, b_w_out, diff_tabs, diff_d)
            xa = _odd_mixer(h, xa, m[2], lw, s_len, l_len, tm, lam_init)
        h2, expert, gate_w = _norm_route(xa, norm_ffn[l], m[3], m[4], s_len, router_w, router_bias)
        f = _moe(h2, expert, gate_w, router_w.shape[1], moe_w1, moe_w3, moe_w2, l)
        xa = _gated_add(xa, f, m[5], s_len)
    out = _rmsnorm_rows(xa, norm_final, s_len)
    return out.reshape(1, s_len, d)
```

```python
import functools
import math

import jax
import jax.numpy as jnp
from jax import lax
from jax.experimental import pallas as pl
from jax.experimental.pallas import tpu as pltpu

F32 = jnp.float32
BF16 = jnp.bfloat16

GRID_W = 64
EPS = 1e-6
NEG_INF = -1e30
ROPE_BASE = 10000.0
N_MOD = 6
A_DK = 256
A_DV = 512
B_VDIM = 128
B_NOPE = 128
B_ROPE = 64
B_Q_RANK = 1024
B_KV_RANK = 512
C_DIM = 128
C_WIN_R = 8
C_WIN_C = 16
D_DIM = 128
D_VDIM = 256
N_GROUPS = 4
TOP_K = 2

LANES = 128
VMEM_LIMIT_BYTES = 56 * 1024 * 1024
MLA_HEAD_PAD = 256

A_CHUNK_K = 256
MOE_ROWS = 512
NAT_ROWS = 8
NAT_UNROLL = 4
FLASH_KEYS = 256


def _pick(n, cands):
    for c in cands:
        if n % c == 0:
            return c
    return n


def _params(sem):
    return pltpu.CompilerParams(dimension_semantics=sem, vmem_limit_bytes=VMEM_LIMIT_BYTES)


def _mm_body(*refs, prologue, has_bias, has_tab, rope_d, has_resid, n_split, tm):
    it = iter(refs)
    a_ref = next(it)
    b_ref = next(it)
    nw_ref = next(it) if prologue == "rmsnorm" else None
    bias_ref = next(it) if has_bias else None
    tab_ref = next(it) if has_tab else None
    if rope_d:
        cos_ref, sa_ref, sb_ref = next(it), next(it), next(it)
    if has_resid:
        res_ref, gate_ref = next(it), next(it)
    o_ref = next(it)

    a = a_ref[...]
    if prologue == "silu":
        af = a.astype(F32)
        a = af * jax.nn.sigmoid(af)
    elif prologue == "rmsnorm":
        af = a.astype(F32)
        a = af * lax.rsqrt(jnp.mean(af * af, axis=-1, keepdims=True) + EPS) * nw_ref[...]
    y = jnp.dot(a.astype(BF16), b_ref[...].astype(BF16), preferred_element_type=F32)
    if has_bias:
        y = y + bias_ref[...]
    if has_tab:
        y = y + tab_ref[...]
    if rope_d:
        w = y.shape[1]
        y = y * cos_ref[...] + pltpu.roll(y, w - rope_d, 1) * sa_ref[...] + pltpu.roll(y, rope_d, 1) * sb_ref[...]
    if has_resid:
        rows = pl.program_id(0) * tm + lax.broadcasted_iota(jnp.int32, (tm, 1), 0)
        g = jnp.where(rows < n_split, gate_ref[0:1, :], gate_ref[1:2, :])
        y = res_ref[...] + g * y
    o_ref[...] = y.astype(o_ref.dtype)


def _mm(a, b, *, n_cols, tm, tn, out_dtype, b_lead=None, b_col0=0, a_k=None, a_col0=0, prologue=None,
        norm_w=None, bias=None, tab=None, rope=None, rope_d=0, resid=None, gate=None, n_split=0, name="mm"):
    m = a.shape[0]
    k = a_k if a_k is not None else a.shape[1]
    assert m % tm == 0 and n_cols % tn == 0 and b_col0 % tn == 0 and a_col0 % k == 0
    acb, bcb = a_col0 // k, b_col0 // tn
    in_specs = [pl.BlockSpec((tm, k), lambda i, j: (i, acb))]
    if b_lead is None:
        in_specs.append(pl.BlockSpec((k, tn), lambda i, j: (0, j + bcb)))
    else:
        in_specs.append(pl.BlockSpec((None, k, tn), lambda i, j: (b_lead, 0, j + bcb)))
    args = [a, b]
    if prologue == "rmsnorm":
        in_specs.append(pl.BlockSpec((1, k), lambda i, j: (0, 0)))
        args.append(norm_w.reshape(1, k).astype(F32))
    if bias is not None:
        in_specs.append(pl.BlockSpec((1, tn), lambda i, j: (0, j)))
        args.append(bias.reshape(1, n_cols).astype(F32))
    if tab is not None:
        in_specs.append(pl.BlockSpec((tm, tn), lambda i, j: (i, 0)))
        args.append(tab)
    if rope is not None:
        for t in rope:
            in_specs.append(pl.BlockSpec((tm, tn), lambda i, j: (i, 0)))
            args.append(t)
    if resid is not None:
        in_specs.append(pl.BlockSpec((tm, tn), lambda i, j: (i, j)))
        in_specs.append(pl.BlockSpec((2, tn), lambda i, j: (0, j)))
        args += [resid, gate]
    body = functools.partial(_mm_body, prologue=prologue, has_bias=bias is not None, has_tab=tab is not None,
                             rope_d=rope_d if rope is not None else 0, has_resid=resid is not None,
                             n_split=n_split, tm=tm)
    return pl.pallas_call(
        body,
        out_shape=jax.ShapeDtypeStruct((m, n_cols), out_dtype),
        grid=(m // tm, n_cols // tn),
        in_specs=in_specs,
        out_specs=pl.BlockSpec((tm, tn), lambda i, j: (i, j)),
        compiler_params=_params(("parallel", "arbitrary")),
        name=name,
    )(*args)


def _norm_mod_body(x_ref, w_ref, sh_ref, sc_ref, o_ref, *, n_split, tm):
    x = x_ref[...]
    y = x * lax.rsqrt(jnp.mean(x * x, axis=-1, keepdims=True) + EPS) * w_ref[...]
    rows = pl.program_id(0) * tm + lax.broadcasted_iota(jnp.int32, (tm, 1), 0)
    lat = rows < n_split
    sc = jnp.where(lat, sc_ref[0:1, :], sc_ref[1:2, :])
    sh = jnp.where(lat, sh_ref[0:1, :], sh_ref[1:2, :])
    o_ref[...] = (y * (1.0 + sc) + sh).astype(o_ref.dtype)


def _norm_mod(x, w, shift, scale, n_split, out_dtype):
    n, d = x.shape
    tm = _pick(n, (256, 128, 64, 32, 16, 8))
    return pl.pallas_call(
        functools.partial(_norm_mod_body, n_split=n_split, tm=tm),
        out_shape=jax.ShapeDtypeStruct((n, d), out_dtype),
        grid=(n // tm,),
        in_specs=[pl.BlockSpec((tm, d), lambda i: (i, 0)),
                  pl.BlockSpec((1, d), lambda i: (0, 0)),
                  pl.BlockSpec((2, d), lambda i: (0, 0)),
                  pl.BlockSpec((2, d), lambda i: (0, 0))],
        out_specs=pl.BlockSpec((tm, d), lambda i: (i, 0)),
        compiler_params=_params(("parallel",)),
        name="norm_mod",
    )(x, w.reshape(1, d), shift, scale)


def _rmsnorm_body(x_ref, w_ref, o_ref):
    x = x_ref[...]
    o_ref[...] = x * lax.rsqrt(jnp.mean(x * x, axis=-1, keepdims=True) + EPS) * w_ref[...]


def _rmsnorm_rows(x, w, n_rows):
    d = x.shape[1]
    tm = _pick(n_rows, (256, 128, 64, 32, 16, 8))
    return pl.pallas_call(
        _rmsnorm_body,
        out_shape=jax.ShapeDtypeStruct((n_rows, d), F32),
        grid=(n_rows // tm,),
        in_specs=[pl.BlockSpec((tm, d), lambda i: (i, 0)), pl.BlockSpec((1, d), lambda i: (0, 0))],
        out_specs=pl.BlockSpec((tm, d), lambda i: (i, 0)),
        compiler_params=_params(("parallel",)),
        name="final_norm",
    )(x, w.reshape(1, d))


def _lane_repeat(x, reps):
    return x if reps == 1 else jnp.concatenate([x] * reps, axis=1)


def _flash_body(q_ref, k_ref, v_ref, o_ref, m_sc, l_sc, acc_sc, *, c_exp, nk, kc, dv_out, l_from_v):
    j = pl.program_id(2)
    tq, tk, dv = q_ref.shape[0], k_ref.shape[0], v_ref.shape[1]

    @pl.when(j == 0)
    def _():
        m_sc[...] = jnp.full(m_sc.shape, -jnp.inf, F32)
        l_sc[...] = jnp.zeros(l_sc.shape, F32)
        acc_sc[...] = jnp.zeros(acc_sc.shape, F32)

    q = q_ref[...]
    m, l, acc = m_sc[...], l_sc[...], acc_sc[...]
    for c in range(tk // kc):
        ks = k_ref[c * kc:(c + 1) * kc, :]
        vs = v_ref[c * kc:(c + 1) * kc, :]
        s = lax.dot_general(q, ks, (((1,), (1,)), ((), ())), preferred_element_type=F32)
        m_new = jnp.maximum(m, jnp.max(s, axis=-1, keepdims=True))
        alpha = jnp.exp2((m - m_new) * c_exp)
        p = jnp.exp2((s - _lane_repeat(m_new, kc // LANES)) * c_exp)
        if not l_from_v:
            l = alpha * l + jnp.sum(p, axis=-1, keepdims=True)
        acc = _lane_repeat(alpha, dv // LANES) * acc + jnp.dot(p.astype(BF16), vs, preferred_element_type=F32)
        m = m_new
    m_sc[...] = m
    if not l_from_v:
        l_sc[...] = l
    acc_sc[...] = acc

    @pl.when(j == nk - 1)
    def _():
        acc = acc_sc[...]
        if l_from_v:
            o = acc[:, :dv_out] / acc[:, dv_out:2 * dv_out]
        else:
            o = acc / _lane_repeat(l_sc[...], dv // LANES)
        o_ref[...] = o.astype(o_ref.dtype)


def _flash_body_rows(q_ref, k_ref, v_ref, o_ref, m_sc, l_sc, acc_sc, *, c_exp, nk, kc, dv_out, l_from_v):
    j = pl.program_id(2)
    tq, tk, dv = q_ref.shape[0], k_ref.shape[0], v_ref.shape[1]
    rsub = min(kc, tq)

    @pl.when(j == 0)
    def _():
        m_sc[...] = jnp.full(m_sc.shape, -jnp.inf, F32)
        l_sc[...] = jnp.zeros(l_sc.shape, F32)
        acc_sc[...] = jnp.zeros(acc_sc.shape, F32)

    k = k_ref[...]
    v = v_ref[...]
    for r in range(tq // rsub):
        sl = slice(r * rsub, (r + 1) * rsub)
        s = lax.dot_general(q_ref[sl, :], k, (((1,), (1,)), ((), ())), preferred_element_type=F32)
        m_prev = m_sc[sl, :]
        m_new = jnp.maximum(m_prev, jnp.max(s, axis=-1, keepdims=True))
        alpha = jnp.exp2((m_prev - m_new) * c_exp)
        p = jnp.exp2((s - _lane_repeat(m_new, tk // LANES)) * c_exp)
        if not l_from_v:
            l_sc[sl, :] = alpha * l_sc[sl, :] + jnp.sum(p, axis=-1, keepdims=True)
        acc_sc[sl, :] = (_lane_repeat(alpha, dv // LANES) * acc_sc[sl, :]
                         + jnp.dot(p.astype(BF16), v, preferred_element_type=F32))
        m_sc[sl, :] = m_new

    @pl.when(j == nk - 1)
    def _():
        acc = acc_sc[...]
        if l_from_v:
            o = acc[:, :dv_out] / acc[:, dv_out:2 * dv_out]
        else:
            o = acc / _lane_repeat(l_sc[...], dv // LANES)
        o_ref[...] = o.astype(o_ref.dtype)


def _flash(q, k, v, *, heads, dk, dv, q_col0, k_col0, v_col0, v_share, q_row0, q_rows, tq, k_row0, k_rows, tk,
           scale, out_dtype, name, l_from_v=False, by_rows=False):
    assert q_rows % tq == 0 and k_rows % tk == 0 and q_row0 % tq == 0 and k_row0 % tk == 0
    assert q_col0 % dk == 0 and k_col0 % dk == 0 and v_col0 % dv == 0 and tk % LANES == 0 and dv % LANES == 0
    qrb, krb, qcb, kcb, vcb = q_row0 // tq, k_row0 // tk, q_col0 // dk, k_col0 // dk, v_col0 // dv
    nq, nk = q_rows // tq, k_rows // tk
    dv_out = dv // 2 if l_from_v else dv
    kc = _pick(tk, (FLASH_KEYS, 128))
    return pl.pallas_call(
        functools.partial(_flash_body_rows if by_rows else _flash_body, c_exp=scale * math.log2(math.e), nk=nk, kc=kc,
                          dv_out=dv_out,
                          l_from_v=l_from_v),
        out_shape=jax.ShapeDtypeStruct((q_rows, heads * dv_out), out_dtype),
        grid=(heads, nq, nk),
        in_specs=[pl.BlockSpec((tq, dk), lambda h, i, j: (i + qrb, h + qcb)),
                  pl.BlockSpec((tk, dk), lambda h, i, j: (j + krb, h + kcb)),
                  pl.BlockSpec((tk, dv), lambda h, i, j: (j + krb, h // v_share + vcb))],
        out_specs=pl.BlockSpec((tq, dv_out), lambda h, i, j: (i, h)),
        scratch_shapes=[pltpu.VMEM((tq, LANES), F32), pltpu.VMEM((tq, LANES), F32), pltpu.VMEM((tq, dv), F32)],
        compiler_params=_params(("parallel", "parallel", "arbitrary")),
        name=name,
    )(q, k, v)


def _diff_post_body(o_ref, lam_ref, w_ref, out_ref, *, lam_init, dv):
    lp = lam_ref[...]
    lam = (jnp.exp(jnp.sum(lp[0:1, :] * lp[1:2, :], axis=-1, keepdims=True))
           - jnp.exp(jnp.sum(lp[2:3, :] * lp[3:4, :], axis=-1, keepdims=True)) + lam_init)
    o = o_ref[...]
    d = o[:, :dv] - lam * o[:, dv:]
    y = d * lax.rsqrt(jnp.mean(d * d, axis=-1, keepdims=True) + EPS) * w_ref[...]
    out_ref[...] = (y * (1.0 - lam_init)).astype(out_ref.dtype)


def _diff_post(o, lam_p, subln, lam_init, heads, dv):
    n = o.shape[0]
    tm = _pick(n, (256, 128, 64, 32, 16, 8))
    return pl.pallas_call(
        functools.partial(_diff_post_body, lam_init=lam_init, dv=dv),
        out_shape=jax.ShapeDtypeStruct((n, heads * dv), BF16),
        grid=(n // tm, heads),
        in_specs=[pl.BlockSpec((tm, 2 * dv), lambda i, h: (i, h)),
                  pl.BlockSpec(lam_p.shape, lambda i, h: (0, 0)),
                  pl.BlockSpec((1, dv), lambda i, h: (0, 0))],
        out_specs=pl.BlockSpec((tm, dv), lambda i, h: (i, h)),
        compiler_params=_params(("parallel", "parallel")),
        name="diff_post",
    )(o, lam_p.astype(F32), subln.reshape(1, dv).astype(F32))


def _natten_body(q_ref, k_ref, v_ref, kc_ref, vc_ref, tab_ref, o_ref, *, scale, rows, kr, wg, rb):
    blk = pl.program_id(1)
    kc = kc_ref[...]
    vc = vc_ref[...]

    def one_row(rr, carry):
        i = blk * rb + rr
        r0 = jnp.clip(i - kr // 2, 0, rows - kr)
        idx0 = r0 - i + (kr - 1)
        q = q_ref[pl.ds(pl.multiple_of(rr * wg, wg), wg), :]
        koff = pl.multiple_of(r0 * wg, wg)
        kw = k_ref[pl.ds(koff, kr * wg), :]
        vw = v_ref[pl.ds(koff, kr * wg), :]
        s_loc = lax.dot_general(q, kw, (((1,), (1,)), ((), ())), preferred_element_type=F32) * scale + tab_ref[idx0]
        s_ctx = lax.dot_general(q, kc, (((1,), (1,)), ((), ())), preferred_element_type=F32) * scale
        m = jnp.maximum(jnp.max(s_loc, axis=-1, keepdims=True), jnp.max(s_ctx, axis=-1, keepdims=True))
        p_loc = jnp.exp(s_loc - m)
        p_ctx = jnp.exp(s_ctx - m)
        l = jnp.sum(p_loc, axis=-1, keepdims=True) + jnp.sum(p_ctx, axis=-1, keepdims=True)
        o = (jnp.dot(p_ctx.astype(BF16), vc, preferred_element_type=F32)
             + jnp.dot(p_loc.astype(BF16), vw, preferred_element_type=F32)) / l
        o_ref[pl.ds(pl.multiple_of(rr * wg, wg), wg), :] = o.astype(o_ref.dtype)
        return carry

    lax.fori_loop(0, rb, one_row, 0, unroll=math.gcd(rb, NAT_UNROLL))


def _natten(p, tab, *, heads, d, q_col0, k_col0, v_col0, s_len, l_len, wg, kr):
    rows = s_len // wg
    rb = _pick(rows, (NAT_ROWS, 4, 2, 1))
    qcb, kcb, vcb = q_col0 // d, k_col0 // d, v_col0 // d
    crb = s_len // l_len
    assert s_len % l_len == 0
    return pl.pallas_call(
        functools.partial(_natten_body, scale=d ** -0.5, rows=rows, kr=kr, wg=wg, rb=rb),
        out_shape=jax.ShapeDtypeStruct((s_len, heads * d), BF16),
        grid=(heads, rows // rb),
        in_specs=[pl.BlockSpec((rb * wg, d), lambda h, i: (i, h + qcb)),
                  pl.BlockSpec((s_len, d), lambda h, i: (0, h + kcb)),
                  pl.BlockSpec((s_len, d), lambda h, i: (0, h + vcb)),
                  pl.BlockSpec((l_len, d), lambda h, i: (crb, h + kcb)),
                  pl.BlockSpec((l_len, d), lambda h, i: (crb, h + vcb)),
                  pl.BlockSpec((None, kr, wg, kr * wg), lambda h, i: (h, 0, 0, 0))],
        out_specs=pl.BlockSpec((rb * wg, d), lambda h, i: (i, h)),
        compiler_params=_params(("parallel", "arbitrary")),
        name="natten",
    )(p, p, p, p, p, tab)


def _natten_table(rpb, wg, kr):
    cols = jnp.arange(wg)
    c0 = jnp.clip(cols - C_WIN_C // 2, 0, wg - C_WIN_C)
    col_ok = (cols[None, :] >= c0[:, None]) & (cols[None, :] < c0[:, None] + C_WIN_C)
    dc_idx = jnp.clip(cols[None, :] - cols[:, None] + C_WIN_C - 1, 0, 2 * C_WIN_C - 2)
    bias_c = rpb.astype(F32)[:, :, dc_idx]
    dr = (jnp.arange(kr)[:, None] - (kr - 1)) + jnp.arange(kr)[None, :] + C_WIN_R - 1
    t = bias_c[:, dr]
    t = jnp.where(col_ok[None, None, None], t, NEG_INF)
    t = t.transpose(0, 1, 3, 2, 4)
    return t.reshape(t.shape[0], kr, wg, kr * wg)


def _log_sigmoid(x):
    return jnp.minimum(x, 0.0) - jnp.log1p(jnp.exp(-jnp.abs(x)))


def _mlstm_body(q_ref, k_ref, v_ref, gr_ref, gc_ref, br_ref, bc_ref, o_ref, ct_sc, n_sc, m_sc, *, t, n_heads, qscale):
    dh = pl.program_id(0)
    c = pl.program_id(1)
    backward = dh >= n_heads

    @pl.when(c == 0)
    def _():
        ct_sc[...] = jnp.zeros(ct_sc.shape, F32)
        n_sc[...] = jnp.zeros(n_sc.shape, F32)
        m_sc[...] = jnp.zeros(m_sc.shape, F32)

    gr = gr_ref[...] + br_ref[...]
    gc = gc_ref[...] + bc_ref[...]
    ig_r, lf_r = gr[0:1, :], _log_sigmoid(gr[1:2, :])
    ig_c, lf_c = gc[:, 0:1], _log_sigmoid(gc[:, 1:2])

    jj = lax.broadcasted_iota(jnp.int32, (t, t), 0)
    ss = lax.broadcasted_iota(jnp.int32, (t, t), 1)
    sgn = jnp.where(backward, -1, 1)
    incl = (ss - jj) * sgn <= 0
    incl_t = (jj - ss) * sgn <= 0
    b_c = jnp.sum(jnp.where(incl, lf_r, 0.0), axis=1, keepdims=True)
    b_r = jnp.sum(jnp.where(incl_t, lf_c, 0.0), axis=0, keepdims=True)
    total = jnp.sum(lf_r, axis=1, keepdims=True)

    m_prev = m_sc[...]
    w_end_r = total - b_r + ig_r
    w_end_c = total - b_c + ig_c
    m_new = jnp.maximum(total + m_prev, jnp.max(w_end_r, axis=1, keepdims=True))
    decay = jnp.exp(total + m_prev - m_new)
    ws_c = jnp.exp(w_end_c - m_new)

    q = q_ref[...] * qscale
    k = k_ref[...]
    v = v_ref[...]
    ct = ct_sc[...]
    n_row = n_sc[...]

    logw = jnp.where(incl, b_c - b_r + ig_r, NEG_INF)
    inter = b_c + m_prev
    m_row = jnp.maximum(inter, jnp.max(logw, axis=1, keepdims=True))
    qk = lax.dot_general(q, k, (((1,), (1,)), ((), ())), preferred_element_type=F32)
    amat = jnp.exp(logw - m_row) * qk
    e_inter = jnp.exp(inter - m_row)
    num = (jnp.dot(amat.astype(BF16), v, preferred_element_type=F32)
           + e_inter * jnp.dot(q, ct.astype(BF16), preferred_element_type=F32))
    qf = q.astype(F32)
    den = jnp.sum(amat, axis=1, keepdims=True) + e_inter * jnp.sum(qf * n_row, axis=1, keepdims=True)
    o_ref[...] = num / jnp.maximum(jnp.abs(den), jnp.exp(-m_row))

    kf = k.astype(F32)
    vws = (v.astype(F32) * ws_c).astype(BF16)
    ct_sc[...] = decay * ct + lax.dot_general(k, vws, (((0,), (0,)), ((), ())), preferred_element_type=F32)
    n_sc[...] = decay * n_row + jnp.sum(kf * ws_c, axis=0, keepdims=True)
    m_sc[...] = m_new


def _mlstm(qkv, gates, gate_bias, *, n_heads, dk, dv, s_len, l_len, t):
    n = s_len + l_len
    nx, nc = s_len // t, l_len // t
    assert s_len % t == 0 and l_len % t == 0
    g4 = gates.reshape(n, 2, 2, n_heads)
    g_sel = g4.transpose(1, 3, 0, 2).reshape(2 * n_heads, n, 2)
    g_col = g_sel
    g_row = g_sel.reshape(2 * n_heads, n // t, t, 2).transpose(0, 1, 3, 2)
    b4 = gate_bias.astype(F32).reshape(2, 2, n_heads).transpose(0, 2, 1).reshape(2 * n_heads, 2)
    b_row = b4.reshape(2 * n_heads, 2, 1)
    b_col = b4.reshape(2 * n_heads, 1, 2)

    def blk(dh, c):
        fwd = jnp.where(c < nc, nx + c, c - nc)
        bwd = jnp.where(c < nc, nx + nc - 1 - c, nx - 1 - (c - nc))
        return jnp.where(dh >= n_heads, bwd, fwd)

    kcb, vcb = n_heads, 2 * n_heads * dk // dv
    return pl.pallas_call(
        functools.partial(_mlstm_body, t=t, n_heads=n_heads, qscale=dk ** -0.5),
        out_shape=jax.ShapeDtypeStruct((2, n, n_heads * dv), F32),
        grid=(2 * n_heads, nx + nc),
        in_specs=[pl.BlockSpec((t, dk), lambda dh, c: (blk(dh, c), dh % n_heads)),
                  pl.BlockSpec((t, dk), lambda dh, c: (blk(dh, c), kcb + dh % n_heads)),
                  pl.BlockSpec((t, dv), lambda dh, c: (blk(dh, c), vcb + dh % n_heads)),
                  pl.BlockSpec((None, None, 2, t), lambda dh, c: (dh, blk(dh, c), 0, 0)),
                  pl.BlockSpec((None, t, 2), lambda dh, c: (dh, blk(dh, c), 0)),
                  pl.BlockSpec((None, 2, 1), lambda dh, c: (dh, 0, 0)),
                  pl.BlockSpec((None, 1, 2), lambda dh, c: (dh, 0, 0))],
        out_specs=pl.BlockSpec((None, t, dv), lambda dh, c: (dh // n_heads, blk(dh, c), dh % n_heads)),
        scratch_shapes=[pltpu.VMEM((dk, dv), F32), pltpu.VMEM((1, dk), F32), pltpu.VMEM((1, 1), F32)],
        compiler_params=_params(("parallel", "arbitrary")),
        name="mlstm",
    )(qkv, qkv, qkv, g_row, g_col, b_row, b_col)


def _mlstm_out_body(hf_ref, hb_ref, o_ref, w_ref, out_ref):
    h = hf_ref[...] + hb_ref[...]
    hn = h * lax.rsqrt(jnp.mean(h * h, axis=-1, keepdims=True) + EPS) * w_ref[...]
    out_ref[...] = (hn * jax.nn.sigmoid(o_ref[...])).astype(out_ref.dtype)


def _mlstm_out(hs, o, w, n_heads, dv):
    n = o.shape[0]
    tm = _pick(n, (256, 128, 64, 32, 16, 8))
    return pl.pallas_call(
        _mlstm_out_body,
        out_shape=jax.ShapeDtypeStruct((n, n_heads * dv), BF16),
        grid=(n // tm, n_heads),
        in_specs=[pl.BlockSpec((None, tm, dv), lambda i, h: (0, i, h)),
                  pl.BlockSpec((None, tm, dv), lambda i, h: (1, i, h)),
                  pl.BlockSpec((tm, dv), lambda i, h: (i, h)),
                  pl.BlockSpec((1, dv), lambda i, h: (0, h))],
        out_specs=pl.BlockSpec((tm, dv), lambda i, h: (i, h)),
        compiler_params=_params(("parallel", "parallel")),
        name="mlstm_out",
    )(hs, hs, o, w.reshape(1, n_heads * dv).astype(F32))


def _route(h, wt_ref, b_ref, e_ref, g_ref, n_exp, n_groups):
    logits = lax.dot_general(wt_ref[...], h, (((1,), (1,)), ((), ())), preferred_element_type=F32,
                             precision=lax.Precision.HIGHEST)
    scores = jax.nn.sigmoid(logits)
    sel = scores + b_ref[...]
    epg = n_exp // n_groups
    sc = [scores[e:e + 1, :] for e in range(n_exp)]
    sl = [sel[e:e + 1, :] for e in range(n_exp)]

    def top2_sum(vals):
        best = jnp.maximum(vals[0], vals[1])
        second = jnp.minimum(vals[0], vals[1])
        for x in vals[2:]:
            second = jnp.maximum(second, jnp.minimum(best, x))
            best = jnp.maximum(best, x)
        return best + second

    gs = [top2_sum(sl[g * epg:(g + 1) * epg]) for g in range(n_groups)]
    g_best, g_idx = gs[0], jnp.zeros_like(gs[0], dtype=jnp.int32)
    for g in range(1, n_groups):
        better = gs[g] > g_best
        g_best = jnp.where(better, gs[g], g_best)
        g_idx = jnp.where(better, g, g_idx)
    in_sel, in_raw = [], []
    for j in range(epg):
        vs, vr = sl[j], sc[j]
        for g in range(1, n_groups):
            vs = jnp.where(g_idx == g, sl[g * epg + j], vs)
            vr = jnp.where(g_idx == g, sc[g * epg + j], vr)
        in_sel.append(vs)
        in_raw.append(vr)
    v1, i1, r1 = in_sel[0], jnp.zeros_like(g_idx), in_raw[0]
    for j in range(1, epg):
        better = in_sel[j] > v1
        v1 = jnp.where(better, in_sel[j], v1)
        i1 = jnp.where(better, j, i1)
        r1 = jnp.where(better, in_raw[j], r1)
    v2 = jnp.full_like(v1, -jnp.inf)
    i2, r2 = jnp.zeros_like(g_idx), jnp.zeros_like(r1)
    for j in range(epg):
        better = (in_sel[j] > v2) & (i1 != j)
        v2 = jnp.where(better, in_sel[j], v2)
        i2 = jnp.where(better, j, i2)
        r2 = jnp.where(better, in_raw[j], r2)
    tot = r1 + r2
    e_ref[...] = jnp.concatenate([g_idx * epg + i1, g_idx * epg + i2], axis=0)
    g_ref[...] = jnp.concatenate([r1 / tot, r2 / tot], axis=0)


def _norm_route_body(x_ref, w_ref, sh_ref, sc_ref, wt_ref, b_ref, o_ref, e_ref, g_ref, *, n_split, tm, n_exp,
                     n_groups):
    x = x_ref[...]
    y = x * lax.rsqrt(jnp.mean(x * x, axis=-1, keepdims=True) + EPS) * w_ref[...]
    rows = pl.program_id(0) * tm + lax.broadcasted_iota(jnp.int32, (tm, 1), 0)
    lat = rows < n_split
    sc = jnp.where(lat, sc_ref[0:1, :], sc_ref[1:2, :])
    sh = jnp.where(lat, sh_ref[0:1, :], sh_ref[1:2, :])
    h = y * (1.0 + sc) + sh
    half = h.shape[1] // 2
    lo = pltpu.bitcast(h[:, :half].astype(BF16).astype(F32), jnp.uint32) >> 16
    hi = pltpu.bitcast(h[:, half:].astype(BF16).astype(F32), jnp.uint32) & jnp.uint32(0xFFFF0000)
    o_ref[...] = lo | hi
    _route(h, wt_ref, b_ref, e_ref, g_ref, n_exp, n_groups)


def _norm_route(x, w, shift, scale, n_split, router_w, router_bias):
    n, d = x.shape
    n_exp = router_w.shape[1]
    tm = _pick(n, (256, 128))
    return pl.pallas_call(
        functools.partial(_norm_route_body, n_split=n_split, tm=tm, n_exp=n_exp, n_groups=N_GROUPS),
        out_shape=(jax.ShapeDtypeStruct((n, d // 2), jnp.uint32),
                   jax.ShapeDtypeStruct((TOP_K, n), jnp.int32), jax.ShapeDtypeStruct((TOP_K, n), F32)),
        grid=(n // tm,),
        in_specs=[pl.BlockSpec((tm, d), lambda i: (i, 0)),
                  pl.BlockSpec((1, d), lambda i: (0, 0)),
                  pl.BlockSpec((2, d), lambda i: (0, 0)),
                  pl.BlockSpec((2, d), lambda i: (0, 0)),
                  pl.BlockSpec((n_exp, d), lambda i: (0, 0)),
                  pl.BlockSpec((n_exp, 1), lambda i: (0, 0))],
        out_specs=(pl.BlockSpec((tm, d // 2), lambda i: (i, 0)),
                   pl.BlockSpec((TOP_K, tm), lambda i: (0, i)), pl.BlockSpec((TOP_K, tm), lambda i: (0, i))),
        compiler_params=_params(("parallel",)),
        name="norm_route",
    )(x, w.reshape(1, d), shift, scale, router_w.T.astype(F32), router_bias.reshape(n_exp, 1).astype(F32))


def _row_gather_copy(src_hbm, row, dst_buf, slot, r, sem):
    return pltpu.make_async_copy(src_hbm.at[pl.ds(row, 1), :], dst_buf.at[slot, pl.ds(r, 1), :], sem.at[slot])


def _experts_body(be_ref, bv_ref, tok_ref, h_hbm, w1_ref, w3_ref, w2_ref, rw_ref, o_ref, x_buf, sem, *, tb, nb):
    b = pl.program_id(0)
    f = pl.program_id(1)
    slot = b % 2

    def gather(blk, sl, start):
        def body(r, carry):
            cp = _row_gather_copy(h_hbm, tok_ref[blk * tb + r], x_buf, sl, r, sem)
            if start:
                cp.start()
            else:
                cp.wait()
            return carry
        lax.fori_loop(0, tb, body, 0)

    @pl.when(f == 0)
    def _():
        o_ref[...] = jnp.zeros(o_ref.shape, F32)

        @pl.when((b == 0) & (bv_ref[0] > 0))
        def _():
            gather(0, 0, True)

        @pl.when(bv_ref[b] > 0)
        def _():
            gather(b, slot, False)

        nxt = jnp.minimum(b + 1, nb - 1)

        @pl.when((b + 1 < nb) & (bv_ref[nxt] > 0))
        def _():
            gather(nxt, 1 - slot, True)

    @pl.when(bv_ref[b] > 0)
    def _():
        xu = x_buf[slot]
        half = xu.shape[1]
        x_lo = pltpu.bitcast(xu << 16, F32).astype(BF16)
        x_hi = pltpu.bitcast(xu & jnp.uint32(0xFFFF0000), F32).astype(BF16)
        w1 = w1_ref[...].astype(BF16)
        w3 = w3_ref[...].astype(BF16)
        h1 = (jnp.dot(x_lo, w1[:half], preferred_element_type=F32)
              + jnp.dot(x_hi, w1[half:], preferred_element_type=F32))
        h3 = (jnp.dot(x_lo, w3[:half], preferred_element_type=F32)
              + jnp.dot(x_hi, w3[half:], preferred_element_type=F32))
        act = (h1 * jax.nn.sigmoid(h1) * h3).astype(BF16)
        y = jnp.dot(act, w2_ref[...].astype(BF16), preferred_element_type=F32)
        o_ref[...] += y * rw_ref[...]


def _experts(h_packed, buf_tok, row_w, blk_e, blk_valid, w1, w3, w2, layer, tb, fc):
    n_rows = buf_tok.shape[0]
    d = w1.shape[-2]
    d_exp = w1.shape[-1]
    nf = d_exp // fc
    nb = n_rows // tb

    def fsel(b, f, bv):
        return jnp.where(bv[b] > 0, f, nf - 1)

    grid_spec = pltpu.PrefetchScalarGridSpec(
        num_scalar_prefetch=3,
        grid=(nb, nf),
        in_specs=[pl.BlockSpec(memory_space=pl.ANY),
                  pl.BlockSpec((None, None, d, fc), lambda b, f, be, bv, tk: (layer, be[b], 0, fsel(b, f, bv))),
                  pl.BlockSpec((None, None, d, fc), lambda b, f, be, bv, tk: (layer, be[b], 0, fsel(b, f, bv))),
                  pl.BlockSpec((None, None, fc, d), lambda b, f, be, bv, tk: (layer, be[b], fsel(b, f, bv), 0)),
                  pl.BlockSpec((tb, 1), lambda b, f, be, bv, tk: (b, 0))],
        out_specs=pl.BlockSpec((tb, d), lambda b, f, be, bv, tk: (b, 0)),
        scratch_shapes=[pltpu.VMEM((2, tb, d // 2), jnp.uint32), pltpu.SemaphoreType.DMA((2,))],
    )
    return pl.pallas_call(
        functools.partial(_experts_body, tb=tb, nb=nb),
        out_shape=jax.ShapeDtypeStruct((n_rows, d), F32),
        grid_spec=grid_spec,
        compiler_params=_params(("arbitrary", "arbitrary")),
        name="experts",
    )(blk_e, blk_valid, buf_tok, h_packed, w1, w3, w2, row_w)


def _combine_body(d_ref, y_hbm, x_ref, g_ref, o_ref, buf, sem, *, tm, nblk, n_split, top_k):
    i = pl.program_id(0)
    slot = i % 2

    def gather(blk, sl, start):
        def body(r, carry):
            cp = _row_gather_copy(y_hbm, d_ref[blk * (top_k * tm) + r], buf, sl, r, sem)
            if start:
                cp.start()
            else:
                cp.wait()
            return carry
        lax.fori_loop(0, top_k * tm, body, 0)

    @pl.when(i == 0)
    def _():
        gather(0, 0, True)

    gather(i, slot, False)

    @pl.when(i + 1 < nblk)
    def _():
        gather(i + 1, 1 - slot, True)

    rows = i * tm + lax.broadcasted_iota(jnp.int32, (tm, 1), 0)
    g = jnp.where(rows < n_split, g_ref[0:1, :], g_ref[1:2, :])
    f = buf[slot, 0:tm, :]
    for kk in range(1, top_k):
        f = f + buf[slot, kk * tm:(kk + 1) * tm, :]
    o_ref[...] = x_ref[...] + g * f


def _combine(x, yb, dest, gate, n_split):
    n, d = x.shape
    top_k = dest.shape[1]
    tm = _pick(n, (128, 64, 32, 16, 8))
    nblk = n // tm
    d_blk = dest.reshape(nblk, tm, top_k).transpose(0, 2, 1).reshape(-1).astype(jnp.int32)
    grid_spec = pltpu.PrefetchScalarGridSpec(
        num_scalar_prefetch=1,
        grid=(nblk,),
        in_specs=[pl.BlockSpec(memory_space=pl.ANY),
                  pl.BlockSpec((tm, d), lambda i, dr: (i, 0)),
                  pl.BlockSpec((2, d), lambda i, dr: (0, 0))],
        out_specs=pl.BlockSpec((tm, d), lambda i, dr: (i, 0)),
        scratch_shapes=[pltpu.VMEM((2, top_k * tm, d), F32), pltpu.SemaphoreType.DMA((2,))],
    )
    return pl.pallas_call(
        functools.partial(_combine_body, tm=tm, nblk=nblk, n_split=n_split, top_k=top_k),
        out_shape=jax.ShapeDtypeStruct((n, d), F32),
        grid_spec=grid_spec,
        compiler_params=_params(("arbitrary",)),
        name="moe_combine",
    )(d_blk, yb, x, gate)


def _moe(xa, h_packed, expert, gate, mod_gate, n_split, n_exp, w1, w3, w2, layer):
    t_tok = xa.shape[0]
    tb = MOE_ROWS
    n_assign = t_tok * TOP_K
    e_flat = expert.T.reshape(-1)
    w_flat = gate.T.reshape(-1)
    onehot = (e_flat[:, None] == jnp.arange(n_exp, dtype=jnp.int32)[None, :]).astype(jnp.int32)
    rank = jnp.take_along_axis(jnp.cumsum(onehot, axis=0) - onehot, e_flat[:, None], axis=1)[:, 0]
    counts = jnp.sum(onehot, axis=0)
    padded = (counts + tb - 1) // tb * tb
    pends = jnp.cumsum(padded)
    dest = (pends - padded)[e_flat] + rank
    nb = (n_assign + tb - 1) // tb + n_exp
    n_rows = nb * tb
    tok = jnp.arange(n_assign, dtype=jnp.int32) // TOP_K
    buf_tok = jnp.zeros((n_rows,), jnp.int32).at[dest].set(tok)
    buf_w = jnp.zeros((n_rows,), F32).at[dest].set(w_flat)
    blk_start = jnp.arange(nb, dtype=jnp.int32) * tb
    blk_e = jnp.minimum(jnp.searchsorted(pends, blk_start, side="right"), n_exp - 1).astype(jnp.int32)
    blk_valid = (blk_start < pends[-1]).astype(jnp.int32)
    fc = _pick(w1.shape[-1], (256, 128))
    yb = _experts(h_packed, buf_tok, buf_w.reshape(n_rows, 1), blk_e, blk_valid, w1, w3, w2, layer, tb, fc)
    return _combine(xa, yb, dest.reshape(t_tok, TOP_K), mod_gate, n_split)


def _axial_tables(s_len, l_len, head_w, pe_off, pe_dim, reps):
    t = jnp.arange(s_len, dtype=jnp.int32)
    pos = jnp.stack([t // GRID_W, t % GRID_W], axis=0).astype(F32)
    h = pe_dim // 2
    d = h // 2
    inv = ROPE_BASE ** (-jnp.arange(d, dtype=F32) / d)
    lane = jnp.arange(pe_dim)
    sub, rr = lane // h, lane % h
    ang = pos[sub, :].T * inv[rr % d][None, :]
    first = (rr < d)[None, :]
    cos = jnp.ones((s_len, head_w), F32).at[:, pe_off:pe_off + pe_dim].set(jnp.cos(ang))
    sa = jnp.zeros((s_len, head_w), F32).at[:, pe_off:pe_off + pe_dim].set(jnp.where(first, -jnp.sin(ang), 0.0))
    sb = jnp.zeros((s_len, head_w), F32).at[:, pe_off:pe_off + pe_dim].set(jnp.where(first, 0.0, jnp.sin(ang)))
    cos = jnp.concatenate([cos, jnp.ones((l_len, head_w), F32)], axis=0)
    sa = jnp.concatenate([sa, jnp.zeros((l_len, head_w), F32)], axis=0)
    sb = jnp.concatenate([sb, jnp.zeros((l_len, head_w), F32)], axis=0)
    return tuple(jnp.tile(a, (1, reps)) for a in (cos, sa, sb)), d


def _even_mixer(h, xa, gate, lw, s_len, l_len, tm):
    (w_in, j, gate_bias, head_norm, q_norm, kv_norm, w_uq, w_ukv, w_out, mla_tabs, mla_d) = lw
    n, d = h.shape
    half = d // 2
    a_heads = half // A_DV
    b_heads = half // B_VDIM
    n_qk = 2 * a_heads * A_DK
    n_v = a_heads * A_DV
    qkv = _mm(h, w_in, b_lead=j, n_cols=n_qk + n_v, tm=tm, tn=_pick(n_qk + n_v, (512, 256, 128)), out_dtype=BF16,
              name="a_in_qkv")
    o_gate = _mm(h, w_in, b_lead=j, b_col0=n_qk + n_v, n_cols=n_v, tm=tm, tn=_pick(n_v, (512, 256, 128)),
                 out_dtype=F32, name="a_in_o")
    off = n_qk + 2 * n_v
    n_g = 4 * a_heads
    w_misc = jnp.concatenate([
        w_in[j][:, off + n_g:off + n_g + B_Q_RANK + B_KV_RANK],
        w_in[j][:, off:off + n_g],
        w_in[j][:, off + n_g + B_Q_RANK + B_KV_RANK:]], axis=1)
    n_misc = w_misc.shape[1]
    n_misc_pad = -(-n_misc // 256) * 256
    w_misc = jnp.pad(w_misc, ((0, 0), (0, n_misc_pad - n_misc)))
    misc = _mm(h, w_misc, n_cols=n_misc_pad, tm=tm, tn=256, out_dtype=F32, name="a_in_misc")
    g0 = B_Q_RANK + B_KV_RANK
    gates = misc[:, g0:g0 + n_g]
    kpe = misc[:, g0 + n_g:g0 + n_g + B_ROPE]

    hs = _mlstm(qkv, gates, gate_bias, n_heads=a_heads, dk=A_DK, dv=A_DV, s_len=s_len, l_len=l_len,
                t=_pick(l_len, (A_CHUNK_K, 128, 64)))
    a_out = _mlstm_out(hs, o_gate, head_norm, a_heads, A_DV)

    hp = MLA_HEAD_PAD
    qd = B_NOPE + B_ROPE
    w_uq_p = jnp.pad(w_uq[j].reshape(B_Q_RANK, b_heads, qd), ((0, 0), (0, 0), (0, hp - qd))).reshape(B_Q_RANK, b_heads * hp)
    w_ukv3 = w_ukv[j].reshape(B_KV_RANK, b_heads, B_NOPE + B_VDIM)
    w_uk_p = jnp.pad(w_ukv3[:, :, :B_NOPE], ((0, 0), (0, 0), (0, hp - B_NOPE))).reshape(B_KV_RANK, b_heads * hp)
    w_uv = jnp.pad(w_ukv3[:, :, B_NOPE:], ((0, 0), (0, 0), (0, B_VDIM))).reshape(B_KV_RANK, b_heads * 2 * B_VDIM)
    v_ones = jnp.tile(jnp.concatenate([jnp.zeros((B_VDIM,), F32), jnp.ones((B_VDIM,), F32)]), b_heads)
    kpe_tab = jnp.tile(jnp.pad(kpe, ((0, 0), (B_NOPE, hp - qd))), (1, 2))
    tn_h = 2 * hp
    q = _mm(misc, w_uq_p, a_k=B_Q_RANK, a_col0=0, n_cols=b_heads * hp, tm=tm, tn=tn_h, out_dtype=BF16,
            prologue="rmsnorm", norm_w=q_norm, rope=mla_tabs, rope_d=mla_d, name="mla_q")
    kk = _mm(misc, w_uk_p, a_k=B_KV_RANK, a_col0=B_Q_RANK, n_cols=b_heads * hp, tm=tm, tn=tn_h, out_dtype=BF16,
             prologue="rmsnorm", norm_w=kv_norm, tab=kpe_tab, rope=mla_tabs, rope_d=mla_d, name="mla_k")
    vv = _mm(misc, w_uv, a_k=B_KV_RANK, a_col0=B_Q_RANK, n_cols=b_heads * 2 * B_VDIM, tm=tm,
             tn=_pick(b_heads * 2 * B_VDIM, (512, 256)), out_dtype=BF16, prologue="rmsnorm", norm_w=kv_norm,
             bias=v_ones, name="mla_v")
    scale = qd ** -0.5
    tq = _pick(s_len, (1024, 512, 256, 128))
    tk = _pick(n, (768, 512, 384, 256, 128))
    b_x = _flash(q, kk, vv, heads=b_heads, dk=hp, dv=2 * B_VDIM, q_col0=0, k_col0=0, v_col0=0, v_share=1, q_row0=0,
                 q_rows=s_len, tq=tq, k_row0=0, k_rows=n, tk=tk, scale=scale, out_dtype=BF16, l_from_v=True,
                 by_rows=(j % 2 == 1), name="mla_attn_x")
    b_c = _flash(q, kk, vv, heads=b_heads, dk=hp, dv=2 * B_VDIM, q_col0=0, k_col0=0, v_col0=0, v_share=1,
                 q_row0=s_len, q_rows=l_len, tq=l_len, k_row0=s_len, k_rows=l_len, tk=l_len, scale=scale,
                 out_dtype=BF16, l_from_v=True, name="mla_attn_c")
    mix = jnp.concatenate([a_out, jnp.concatenate([b_x, b_c], axis=0)], axis=1)
    return _mm(mix, w_out, b_lead=j, n_cols=d, tm=tm, tn=_pick(d, (512, 256, 128)), out_dtype=F32, resid=xa,
               gate=gate, n_split=s_len, name="a_out")


def _odd_mixer(h, xa, gate, lw, s_len, l_len, tm, lam_init):
    (w_in, j, rpb, lam_p, subln, w_out, diff_tabs, diff_d) = lw
    n, d = h.shape
    half = d // 2
    c_heads = half // C_DIM
    d_heads = half // D_VDIM
    n_c = c_heads * C_DIM
    n_dq = 2 * d_heads * D_DIM
    n_dv = d_heads * D_VDIM
    tn = _pick(math.gcd(math.gcd(3 * n_c, 2 * n_dq), n_dv), (512, 256, 128))
    nat = _mm(h, w_in, b_lead=j, n_cols=3 * n_c, tm=tm, tn=tn, out_dtype=BF16, name="b_in_nat")
    dqk = _mm(h, w_in, b_lead=j, b_col0=3 * n_c, n_cols=2 * n_dq, tm=tm, tn=tn, out_dtype=BF16,
              rope=tuple(jnp.tile(tb, (1, tn // D_DIM)) for tb in diff_tabs), rope_d=diff_d, name="b_in_dqk")
    dvv = _mm(h, w_in, b_lead=j, b_col0=3 * n_c + 2 * n_dq, n_cols=n_dv, tm=tm, tn=tn, out_dtype=BF16,
              name="b_in_dv")

    rows = s_len // GRID_W
    kr = min(C_WIN_R, rows)
    tab = _natten_table(rpb[j], GRID_W, kr)
    na_x = _natten(nat, tab, heads=c_heads, d=C_DIM, q_col0=0, k_col0=n_c, v_col0=2 * n_c, s_len=s_len, l_len=l_len,
                   wg=GRID_W, kr=kr)
    na_c = _flash(nat, nat, nat, heads=c_heads, dk=C_DIM, dv=C_DIM, q_col0=0, k_col0=n_c, v_col0=2 * n_c, v_share=1,
                  q_row0=s_len, q_rows=l_len, tq=l_len, k_row0=s_len, k_rows=l_len, tk=l_len, scale=C_DIM ** -0.5,
                  out_dtype=BF16, name="nat_attn_c")

    tq = _pick(s_len, (1024, 512, 256, 128))
    tk = _pick(n, (768, 512, 384, 256, 128))
    maps = 2 * d_heads
    df_x = _flash(dqk, dqk, dvv, heads=maps, dk=D_DIM, dv=D_VDIM, q_col0=0, k_col0=n_dq, v_col0=0, v_share=2,
                  q_row0=0, q_rows=s_len, tq=tq, k_row0=0, k_rows=n, tk=tk, scale=D_DIM ** -0.5, out_dtype=F32,
                  by_rows=(j % 2 == 1), name="diff_attn_x")
    df_c = _flash(dqk, dqk, dvv, heads=maps, dk=D_DIM, dv=D_VDIM, q_col0=0, k_col0=n_dq, v_col0=0, v_share=2,
                  q_row0=s_len, q_rows=l_len, tq=l_len, k_row0=s_len, k_rows=l_len, tk=l_len, scale=D_DIM ** -0.5,
                  out_dtype=F32, name="diff_attn_c")
    df = _diff_post(jnp.concatenate([df_x, df_c], axis=0), lam_p[j], subln[j], lam_init, d_heads, D_VDIM)
    mix = jnp.concatenate([jnp.concatenate([na_x, na_c], axis=0), df], axis=1)
    return _mm(mix, w_out, b_lead=j, n_cols=d, tm=tm, tn=_pick(d, (512, 256, 128)), out_dtype=F32, resid=xa,
               gate=gate, n_split=s_len, name="b_out")


def kernel(x, c, ctx, c_ctx, ada_down, ada_up, ada_bias, norm_mix, norm_ffn, norm_final, a_w_in, a_gate_bias,
           a_head_norm, a_q_norm, a_kv_norm, a_w_uq, a_w_ukv, a_w_out, b_w_in, b_rpb, b_lambda, b_subln, b_w_out,
           router_w, router_bias, moe_w1, moe_w3, moe_w2):
    bsz, s_len, d = x.shape
    l_len = ctx.shape[1]
    depth = ada_down.shape[0]
    assert bsz == 1
    n = s_len + l_len
    xa = jnp.concatenate([x[0], ctx[0]], axis=0)
    tm = _pick(n, (768, 512, 384, 256, 128))

    mla_tabs, mla_d = _axial_tables(s_len, l_len, MLA_HEAD_PAD, B_NOPE, B_ROPE, 2)
    diff_tabs, diff_d = _axial_tables(s_len, l_len, D_DIM, 0, D_DIM, 1)

    cvec = jnp.zeros((16, d), F32).at[0].set(c[0]).at[1].set(c_ctx)
    rank = ada_down.shape[-1]
    n_mod_cols = ada_up.shape[-1]
    for l in range(depth):
        j = l // 2
        t_low = _mm(cvec, ada_down, b_lead=l, n_cols=rank, tm=16, tn=rank, out_dtype=F32, prologue="silu",
                    name="ada_down")
        mods = _mm(t_low, ada_up, b_lead=l, n_cols=n_mod_cols, tm=16, tn=_pick(n_mod_cols, (2048, 1024, 512, 256, 128)),
                   out_dtype=F32, bias=ada_bias[l], name="ada_up")
        mods = mods[:2].reshape(2, N_MOD, d)
        m = [mods[:, i, :] for i in range(N_MOD)]
        h = _norm_mod(xa, norm_mix[l], m[0], m[1], s_len, BF16)
        if l % 2 == 0:
            lw = (a_w_in, j, a_gate_bias[j], a_head_norm[j], a_q_norm[j], a_kv_norm[j], a_w_uq, a_w_ukv, a_w_out,
                  mla_tabs, mla_d)
            xa = _even_mixer(h, xa, m[2], lw, s_len, l_len, tm)
        else:
            lam_init = 0.8 - 0.6 * math.exp(-0.3 * l)
            lw = (b_w_in, j, b_rpb, b_lambda, b_subln, b_w_out, diff_tabs, diff_d)
            xa = _odd_mixer(h, xa, m[2], lw, s_len, l_len, tm, lam_init)
        h2, expert, gate_w = _norm_route(xa, norm_ffn[l], m[3], m[4], s_len, router_w, router_bias)
        xa = _moe(xa, h2, expert, gate_w, m[5], s_len, router_w.shape[1], moe_w1, moe_w3, moe_w2, l)
    out = _rmsnorm_rows(xa, norm_final, s_len)
    return out.reshape(1, s_len, d)
```
